```python
import jax, jax.numpy as jnp
from jax import lax
import numpy as np

D_MODEL = 1024
BATCH = 16
SEQ = 2048
DEPTH = 1

POOL_WINDOWS = (2, 4, 8, 16)
POOL_WIDTH = D_MODEL // 2
POOL_GROUP = POOL_WIDTH // len(POOL_WINDOWS)
N_HEADS = 8
QK_NOPE = 64
QK_ROPE = 32
QK_HEAD = QK_NOPE + QK_ROPE
V_HEAD = 64
ATTN_WIDTH = N_HEADS * V_HEAD
Q_LORA = D_MODEL // 4
KV_LORA = D_MODEL // 8
MIX_WIDTH = POOL_WIDTH + ATTN_WIDTH
IN_SPLITS = (POOL_WIDTH, POOL_WIDTH, Q_LORA, KV_LORA, QK_ROPE, ATTN_WIDTH)
IN_WIDTH = sum(IN_SPLITS)
ROPE_THETA = 10000.0
Q_BLOCK = 128
EPS = 1e-6

kernel_name = "hybrid_pool_mla_adaln_encoder_layer"


def rmsnorm(x, g):
    xf = x.astype(jnp.float32)
    r = lax.rsqrt(jnp.mean(xf * xf, axis=-1, keepdims=True) + EPS)
    return (xf * r).astype(x.dtype) * g


def rope_tables(positions):
    inv = ROPE_THETA ** (-jnp.arange(0, QK_ROPE, 2, dtype=jnp.float32) / QK_ROPE)
    ang = positions.astype(jnp.float32)[..., None] * inv
    return jnp.cos(ang)[:, :, None, :], jnp.sin(ang)[:, :, None, :]


def apply_rope(x, cos, sin):
    x1, x2 = jnp.split(x, 2, axis=-1)
    cos = cos.astype(x.dtype)
    sin = sin.astype(x.dtype)
    return jnp.concatenate([x1 * cos - x2 * sin, x2 * cos + x1 * sin], axis=-1)


def multiscale_pool(u, pool_w, pool_scale):
    B, S, P = u.shape
    uf = u.astype(jnp.float32)
    cs = jnp.concatenate([jnp.zeros((B, 1, P), jnp.float32), jnp.cumsum(uf, axis=1)], axis=1)
    t = jnp.arange(S)
    outs = []
    for gi, w in enumerate(POOL_WINDOWS):
        lo = jnp.clip(t - w // 2, 0, S)
        hi = jnp.clip(t + (w - w // 2), 0, S)
        csg = cs[..., gi * POOL_GROUP:(gi + 1) * POOL_GROUP]
        cnt = (hi - lo).astype(jnp.float32)[None, :, None]
        mean = (jnp.take(csg, hi, axis=1) - jnp.take(csg, lo, axis=1)) / cnt
        diff = (mean - uf[..., gi * POOL_GROUP:(gi + 1) * POOL_GROUP]).astype(u.dtype)
        outs.append(jnp.einsum('bsc,cd->bsd', diff, pool_w[gi]))
    return jnp.concatenate(outs, axis=-1) * pool_scale


def latent_attention(q_lat, kv_lat, k_rope_raw, cos, sin, g_q_lat, w_uq, g_kv_lat, w_ukv, g_qnorm, g_knorm):
    B, S, _ = q_lat.shape
    q = jnp.einsum('bsr,rd->bsd', rmsnorm(q_lat, g_q_lat), w_uq).reshape(B, S, N_HEADS, QK_HEAD)
    kv = jnp.einsum('bsr,rd->bsd', rmsnorm(kv_lat, g_kv_lat), w_ukv).reshape(B, S, N_HEADS, QK_NOPE + V_HEAD)
    k_nope, v = kv[..., :QK_NOPE], kv[..., QK_NOPE:]
    k_rope = jnp.broadcast_to(k_rope_raw[:, :, None, :], (B, S, N_HEADS, QK_ROPE))
    k = jnp.concatenate([k_nope, k_rope], axis=-1)
    q = rmsnorm(q, g_qnorm)
    k = rmsnorm(k, g_knorm)
    q = jnp.concatenate([q[..., :QK_NOPE], apply_rope(q[..., QK_NOPE:], cos, sin)], axis=-1)
    k = jnp.concatenate([k[..., :QK_NOPE], apply_rope(k[..., QK_NOPE:], cos, sin)], axis=-1)
    scale = QK_HEAD ** -0.5
    nblk = S // Q_BLOCK
    qb = q.reshape(B, nblk, Q_BLOCK, N_HEADS, QK_HEAD).transpose(1, 0, 2, 3, 4)

    def attend(qblk):
        s = jnp.einsum('bqhd,bkhd->bhqk', qblk, k, preferred_element_type=jnp.float32) * scale
        p = jax.nn.softmax(s, axis=-1)
        return jnp.einsum('bhqk,bkhd->bqhd', p.astype(v.dtype), v)

    o = lax.map(attend, qb)
    return o.transpose(1, 0, 2, 3, 4).reshape(B, S, ATTN_WIDTH)


def setup_inputs(seed: int = 0) -> dict:
    key = jax.random.key(seed)
    ks = jax.random.split(key, 20)
    L, D = DEPTH, D_MODEL
    nrm = lambda k, shape, fan: jax.random.normal(k, shape, jnp.float32) * fan ** -0.5
    gain = lambda k, shape: 1.0 + 0.05 * jax.random.normal(k, shape, jnp.float32)
    return {
        "x": jax.random.normal(ks[0], (BATCH, SEQ, D), jnp.float32),
        "c": jax.random.normal(ks[1], (BATCH, D), jnp.float32),
        "positions": jnp.tile(jnp.arange(SEQ, dtype=jnp.int32)[None, :], (BATCH, 1)),
        "ada_w": nrm(ks[2], (L, D, 3 * D), D) * 0.5,
        "ada_b": 0.02 * jax.random.normal(ks[3], (L, 3 * D), jnp.float32),
        "norm_g": gain(ks[4], (L, D)),
        "w_in": nrm(ks[5], (L, D, IN_WIDTH), D),
        "pool_w": nrm(ks[6], (L, len(POOL_WINDOWS), POOL_GROUP, POOL_GROUP), POOL_GROUP),
        "pool_scale": gain(ks[7], (L, POOL_WIDTH)),
        "g_q_lat": gain(ks[8], (L, Q_LORA)),
        "w_uq": nrm(ks[9], (L, Q_LORA, N_HEADS * QK_HEAD), Q_LORA),
        "g_kv_lat": gain(ks[10], (L, KV_LORA)),
        "w_ukv": nrm(ks[11], (L, KV_LORA, N_HEADS * (QK_NOPE + V_HEAD)), KV_LORA),
        "g_qnorm": gain(ks[12], (L, QK_HEAD)),
        "g_knorm": gain(ks[13], (L, QK_HEAD)),
        "w_out": nrm(ks[14], (L, MIX_WIDTH, D), MIX_WIDTH),
    }


def reference(x, c, positions, ada_w, ada_b, norm_g, w_in, pool_w, pool_scale, g_q_lat, w_uq,
              g_kv_lat, w_ukv, g_qnorm, g_knorm, w_out):
    cos, sin = rope_tables(positions)
    offs = list(np.cumsum(IN_SPLITS)[:-1])
    h = x
    c_act = jax.nn.silu(c)
    for l in range(DEPTH):
        mod = jnp.einsum('bd,de->be', c_act, ada_w[l]) + ada_b[l]
        shift, scale, gate = jnp.split(mod, 3, axis=-1)
        xn = rmsnorm(h, norm_g[l]) * (1.0 + scale[:, None, :]) + shift[:, None, :]
        proj = jnp.einsum('bsd,de->bse', xn, w_in[l])
        u_pool, g_pool, q_lat, kv_lat, k_rope_raw, g_attn = jnp.split(proj, offs, axis=-1)
        y_pool = multiscale_pool(u_pool, pool_w[l], pool_scale[l]) * jax.nn.silu(g_pool)
        y_attn = latent_attention(q_lat, kv_lat, k_rope_raw, cos, sin, g_q_lat[l], w_uq[l],
                                  g_kv_lat[l], w_ukv[l], g_qnorm[l], g_knorm[l]) * jax.nn.silu(g_attn)
        y = jnp.einsum('bsm,md->bsd', jnp.concatenate([y_pool, y_attn], axis=-1), w_out[l])
        h = h + gate[:, None, :] * y
    return h
```

```python
import functools

import jax
import jax.numpy as jnp
import numpy as np
from jax import lax
from jax.experimental import pallas as pl
from jax.experimental.pallas import tpu as pltpu

D_MODEL = 1024
POOL_WINDOWS = (2, 4, 8, 16)
POOL_WIDTH = 512
POOL_GROUP = 128
N_HEADS = 8
QK_NOPE = 64
QK_ROPE = 32
QK_HEAD = 96
V_HEAD = 64
ATTN_WIDTH = 512
Q_LORA = 256
KV_LORA = 128
IN_SPLITS = (512, 512, 256, 128, 32, 512)
ROPE_THETA = 10000.0
EPS = 1e-6

LANES = 128
HEAD_PAD = LANES
POOL_HALO = 8
IN_PAD_WIDTH = 2048
VMEM_LIMIT = 56 * 1024 * 1024

_U0, _G0, _Q0, _KV0, _KR0, _GA0 = 0, 512, 1024, 1280, 1408, 1536


def _silu(x):
    return x * jax.nn.sigmoid(x)


def _rope_kernel(pos_ref, inv_ref, cos_ref, sin_ref):
    ang = pos_ref[...].astype(jnp.float32) * inv_ref[...]
    cos_ref[...] = jnp.cos(ang)
    sin_ref[...] = jnp.sin(ang)


def _rope_tables(positions):
    b, s = positions.shape
    half = QK_ROPE // 2
    inv = ROPE_THETA ** (-jnp.arange(0, QK_ROPE, 2, dtype=jnp.float32) / QK_ROPE)
    rows = b * s * half // LANES
    pos_rep = jnp.repeat(positions.reshape(-1), half).reshape(rows, LANES)
    inv_t = jnp.tile(inv, LANES // half).reshape(1, LANES)
    blk = rows // 8
    cos_c, sin_c = pl.pallas_call(
        _rope_kernel,
        grid=(rows // blk,),
        in_specs=[pl.BlockSpec((blk, LANES), lambda i: (i, 0)),
                  pl.BlockSpec((1, LANES), lambda i: (0, 0))],
        out_specs=[pl.BlockSpec((blk, LANES), lambda i: (i, 0)),
                   pl.BlockSpec((blk, LANES), lambda i: (i, 0))],
        out_shape=[jax.ShapeDtypeStruct((rows, LANES), jnp.float32)] * 2,
        name="rope_tables",
    )(pos_rep, inv_t)
    cos = cos_c.reshape(b, s, half)
    sin = sin_c.reshape(b, s, half)
    ones = jnp.ones((b, s, QK_NOPE), jnp.float32)
    tail1 = jnp.ones((b, s, HEAD_PAD - QK_HEAD), jnp.float32)
    zeros = jnp.zeros((b, s, QK_NOPE), jnp.float32)
    tail0 = jnp.zeros((b, s, HEAD_PAD - QK_HEAD), jnp.float32)
    cosf = jnp.concatenate([ones, cos, cos, tail1], axis=-1)
    sinf = jnp.concatenate([zeros, -sin, sin, tail0], axis=-1)
    return cosf, sinf


def _mod_kernel(c_ref, w_ref, b_ref, o_ref):
    c_act = _silu(c_ref[...])
    o_ref[...] = jnp.dot(c_act, w_ref[...], preferred_element_type=jnp.float32,
                         precision=lax.Precision.HIGHEST) + b_ref[...]


def _ada_mod(c, ada_w, ada_b):
    b, d = c.shape
    n = ada_w.shape[1]
    tn = 512
    return pl.pallas_call(
        _mod_kernel,
        grid=(n // tn,),
        in_specs=[pl.BlockSpec((b, d), lambda j: (0, 0)),
                  pl.BlockSpec((d, tn), lambda j: (0, j)),
                  pl.BlockSpec((1, tn), lambda j: (0, j))],
        out_specs=pl.BlockSpec((b, tn), lambda j: (0, j)),
        out_shape=jax.ShapeDtypeStruct((b, n), jnp.float32),
        name="ada_mod",
    )(c, ada_w, ada_b.reshape(1, n))


def _rms(x, n):
    return lax.rsqrt(jnp.sum(x * x, axis=-1, keepdims=True) * (1.0 / n) + EPS)


def _head_norm_rope(xh, gain, cosf, sinf, low_half):
    xn = xh * _rms(xh, QK_HEAD) * gain
    rot = jnp.where(low_half, pltpu.roll(xn, HEAD_PAD - QK_ROPE // 2, 1),
                    pltpu.roll(xn, QK_ROPE // 2, 1))
    return xn * cosf + rot * sinf


def _proj_kernel(x_ref, mod_ref, ng_ref, win_ref, gq_ref, wuq_ref, gkv_ref, wuk_ref, wuv_ref,
                 gqn_ref, gkn_ref, cos_ref, sin_ref,
                 u_ref, gp_ref, q_ref, k_ref, v_ref, ga_ref):
    x = x_ref[0]
    shift = mod_ref[0, 0:1, :]
    scale = mod_ref[0, 1:2, :]
    xn = x * _rms(x, D_MODEL) * ng_ref[...]
    xn = xn * (1.0 + scale) + shift
    proj = jnp.dot(xn.astype(jnp.bfloat16), win_ref[...], preferred_element_type=jnp.float32)
    u_ref[0] = proj[:, _U0:_U0 + POOL_WIDTH]
    gp_ref[0] = proj[:, _G0:_G0 + POOL_WIDTH]
    ga_ref[0] = proj[:, _GA0:_GA0 + ATTN_WIDTH]
    q_lat = proj[:, _Q0:_Q0 + Q_LORA]
    kv_lat = proj[:, _KV0:_KV0 + KV_LORA]
    kr_blk = proj[:, _KR0:_KR0 + HEAD_PAD]

    qn = (q_lat * _rms(q_lat, Q_LORA) * gq_ref[...]).astype(jnp.bfloat16)
    kvn = (kv_lat * _rms(kv_lat, KV_LORA) * gkv_ref[...]).astype(jnp.bfloat16)
    q_raw = jnp.dot(qn, wuq_ref[...], preferred_element_type=jnp.float32)
    k_nope = jnp.dot(kvn, wuk_ref[...], preferred_element_type=jnp.float32)
    v_ref[0] = jnp.dot(kvn, wuv_ref[...], preferred_element_type=jnp.float32).astype(jnp.bfloat16)

    cosf = cos_ref[0]
    sinf = sin_ref[0]
    lane = lax.broadcasted_iota(jnp.int32, (1, HEAD_PAD), 1)
    low_half = lane < QK_NOPE + QK_ROPE // 2
    sm_scale = QK_HEAD ** -0.5
    for h in range(N_HEADS):
        sl = slice(h * HEAD_PAD, (h + 1) * HEAD_PAD)
        qh = _head_norm_rope(q_raw[:, sl], gqn_ref[...], cosf, sinf, low_half)
        q_ref[0, :, sl] = (qh * sm_scale).astype(jnp.bfloat16)
        kh = _head_norm_rope(k_nope[:, sl] + kr_blk, gkn_ref[...], cosf, sinf, low_half)
        k_ref[0, :, sl] = kh.astype(jnp.bfloat16)


def _proj_stage(x, mod3, norm_g, w_in_p, g_q_lat, w_uq_p, g_kv_lat, w_uk_p, w_uv_p,
                g_qn_p, g_kn_p, cosf, sinf, tm):
    b, s, d = x.shape
    hp = N_HEADS * HEAD_PAD
    tok = lambda w: pl.BlockSpec((1, tm, w), lambda bi, i: (bi, i, 0))
    full = lambda a: pl.BlockSpec(a.shape, lambda bi, i: (0,) * a.ndim)
    return pl.pallas_call(
        _proj_kernel,
        grid=(b, s // tm),
        in_specs=[tok(d), pl.BlockSpec((1, 3, d), lambda bi, i: (bi, 0, 0)), full(norm_g),
                  full(w_in_p), full(g_q_lat), full(w_uq_p), full(g_kv_lat), full(w_uk_p),
                  full(w_uv_p), full(g_qn_p), full(g_kn_p), tok(HEAD_PAD), tok(HEAD_PAD)],
        out_specs=[tok(POOL_WIDTH), tok(POOL_WIDTH), tok(hp), tok(hp), tok(hp), tok(ATTN_WIDTH)],
        out_shape=[jax.ShapeDtypeStruct((b, s, POOL_WIDTH), jnp.float32),
                   jax.ShapeDtypeStruct((b, s, POOL_WIDTH), jnp.float32),
                   jax.ShapeDtypeStruct((b, s, hp), jnp.bfloat16),
                   jax.ShapeDtypeStruct((b, s, hp), jnp.bfloat16),
                   jax.ShapeDtypeStruct((b, s, hp), jnp.bfloat16),
                   jax.ShapeDtypeStruct((b, s, ATTN_WIDTH), jnp.float32)],
        compiler_params=pltpu.CompilerParams(
            dimension_semantics=("arbitrary", "arbitrary"), vmem_limit_bytes=VMEM_LIMIT),
        name="proj_stage",
    )(x, mod3, norm_g, w_in_p, g_q_lat, w_uq_p, g_kv_lat, w_uk_p, w_uv_p, g_qn_p, g_kn_p,
      cosf, sinf)


def _window_sum(u, w):
    acc = u + pltpu.roll(u, 1, 0)
    span = 2
    n = u.shape[0]
    while span < w:
        half = span // 2
        acc = pltpu.roll(acc, n - half, 0) + pltpu.roll(acc, half, 0)
        span *= 2
    return acc


def _pool_kernel(u_ref, g_ref, w_ref, sc_ref, o_ref):
    s = u_ref.shape[1]
    t = lax.broadcasted_iota(jnp.int32, (s, 1), 0)
    zpad = jnp.zeros((POOL_HALO, POOL_GROUP), jnp.float32)
    for gi, w in enumerate(POOL_WINDOWS):
        sl = slice(gi * POOL_GROUP, (gi + 1) * POOL_GROUP)
        u = u_ref[0, :, sl]
        up = jnp.concatenate([zpad, u, zpad], axis=0)
        wsum = _window_sum(up, w)[POOL_HALO:POOL_HALO + s]
        lo = jnp.maximum(t - w // 2, 0)
        hi = jnp.minimum(t + (w - w // 2), s)
        cnt = (hi - lo).astype(jnp.float32)
        diff = wsum / cnt - u
        y = jnp.dot(diff.astype(jnp.bfloat16), w_ref[gi], preferred_element_type=jnp.float32)
        y = y * sc_ref[:, sl] * _silu(g_ref[0, :, sl])
        o_ref[0, :, sl] = y.astype(jnp.bfloat16)


def _pool_stage(u_pool, g_pool, pool_w_bf, pool_scale):
    b, s, p = u_pool.shape
    seq = pl.BlockSpec((1, s, p), lambda bi: (bi, 0, 0))
    return pl.pallas_call(
        _pool_kernel,
        grid=(b,),
        in_specs=[seq, seq, pl.BlockSpec(pool_w_bf.shape, lambda bi: (0, 0, 0)),
                  pl.BlockSpec((1, p), lambda bi: (0, 0))],
        out_specs=seq,
        out_shape=jax.ShapeDtypeStruct((b, s, p), jnp.bfloat16),
        compiler_params=pltpu.CompilerParams(
            dimension_semantics=("arbitrary",), vmem_limit_bytes=VMEM_LIMIT),
        name="pool_stage",
    )(u_pool, g_pool, pool_w_bf, pool_scale)


def _attn_kernel(q_ref, k_ref, v_ref, ga_ref, yp_ref, x_ref, mod_ref, wo_ref, o_ref):
    outs = []
    for h in range(N_HEADS):
        sl = slice(h * HEAD_PAD, (h + 1) * HEAD_PAD)
        qh = q_ref[0, :, sl]
        kh = k_ref[0, :, sl]
        s = lax.dot_general(qh, kh, (((1,), (1,)), ((), ())),
                            preferred_element_type=jnp.float32)
        m = jnp.max(s, axis=-1, keepdims=True)
        p = jnp.exp(s - m)
        l = jnp.sum(p, axis=-1, keepdims=True)
        oh = jnp.dot(p.astype(jnp.bfloat16), v_ref[0, :, sl],
                     preferred_element_type=jnp.float32)
        outs.append((oh / l)[:, :V_HEAD])
    o = jnp.concatenate(outs, axis=-1)
    ya = (o * _silu(ga_ref[0])).astype(jnp.bfloat16)
    ycat = jnp.concatenate([yp_ref[0], ya], axis=-1)
    y = jnp.dot(ycat, wo_ref[...], preferred_element_type=jnp.float32)
    gate = mod_ref[0, 2:3, :]
    o_ref[0] = x_ref[0] + gate * y


def _attn_stage(q, k, v, g_attn, y_pool, x, mod3, w_out_bf, tq):
    b, s, d = x.shape
    hp = N_HEADS * HEAD_PAD
    tok = lambda w: pl.BlockSpec((1, tq, w), lambda bi, i: (bi, i, 0))
    seq = lambda w: pl.BlockSpec((1, s, w), lambda bi, i: (bi, 0, 0))
    return pl.pallas_call(
        _attn_kernel,
        grid=(b, s // tq),
        in_specs=[tok(hp), seq(hp), seq(hp), tok(ATTN_WIDTH), tok(POOL_WIDTH), tok(d),
                  pl.BlockSpec((1, 3, d), lambda bi, i: (bi, 0, 0)),
                  pl.BlockSpec(w_out_bf.shape, lambda bi, i: (0, 0))],
        out_specs=tok(d),
        out_shape=jax.ShapeDtypeStruct((b, s, d), jnp.float32),
        compiler_params=pltpu.CompilerParams(
            dimension_semantics=("arbitrary", "arbitrary"), vmem_limit_bytes=VMEM_LIMIT),
        name="attn_out_stage",
    )(q, k, v, g_attn, y_pool, x, mod3, w_out_bf)


def _pad_heads(w, width):
    kdim = w.shape[0]
    w = w.reshape(kdim, N_HEADS, width)
    w = jnp.pad(w, ((0, 0), (0, 0), (0, HEAD_PAD - width)))
    return w.reshape(kdim, N_HEADS * HEAD_PAD)


def kernel(x, c, positions, ada_w, ada_b, norm_g, w_in, pool_w, pool_scale, g_q_lat, w_uq,
           g_kv_lat, w_ukv, g_qnorm, g_knorm, w_out):
    b, s, d = x.shape
    bf = jnp.bfloat16
    cosf, sinf = _rope_tables(positions)
    mod3 = _ada_mod(c, ada_w[0], ada_b[0]).reshape(b, 3, d)

    wi = w_in[0]
    kr0 = sum(IN_SPLITS[:4])
    w_in_p = jnp.concatenate(
        [wi[:, :kr0], jnp.zeros((d, QK_NOPE), wi.dtype), wi[:, kr0:kr0 + QK_ROPE],
         jnp.zeros((d, HEAD_PAD - QK_HEAD), wi.dtype), wi[:, kr0 + QK_ROPE:]], axis=1).astype(bf)
    w_uq_p = _pad_heads(w_uq[0], QK_HEAD).astype(bf)
    wkv = w_ukv[0].reshape(KV_LORA, N_HEADS, QK_NOPE + V_HEAD)
    w_uk_p = _pad_heads(wkv[..., :QK_NOPE].reshape(KV_LORA, -1), QK_NOPE).astype(bf)
    w_uv_p = _pad_heads(wkv[..., QK_NOPE:].reshape(KV_LORA, -1), V_HEAD).astype(bf)
    g_qn_p = jnp.pad(g_qnorm[0], (0, HEAD_PAD - QK_HEAD)).reshape(1, HEAD_PAD)
    g_kn_p = jnp.pad(g_knorm[0], (0, HEAD_PAD - QK_HEAD)).reshape(1, HEAD_PAD)

    u_pool, g_pool, q, k, v, g_attn = _proj_stage(
        x, mod3, norm_g[0].reshape(1, d), w_in_p, g_q_lat[0].reshape(1, Q_LORA), w_uq_p,
        g_kv_lat[0].reshape(1, KV_LORA), w_uk_p, w_uv_p, g_qn_p, g_kn_p, cosf, sinf, tm=512)
    y_pool = _pool_stage(u_pool, g_pool, pool_w[0].astype(bf), pool_scale[0].reshape(1, POOL_WIDTH))
    return _attn_stage(q, k, v, g_attn, y_pool, x, mod3, w_out[0].astype(bf), tq=256)
```

```python
import math

import jax
import jax.numpy as jnp
from jax import lax
from jax.experimental import pallas as pl
from jax.experimental.pallas import tpu as pltpu

D_MODEL = 1024
POOL_WINDOWS = (2, 4, 8, 16)
POOL_WIDTH = 512
POOL_GROUP = 128
N_HEADS = 8
QK_NOPE = 64
QK_ROPE = 32
QK_HEAD = 96
V_HEAD = 64
ATTN_WIDTH = 512
Q_LORA = 256
KV_LORA = 128
IN_SPLITS = (512, 512, 256, 128, 32, 512)
ROPE_THETA = 10000.0
EPS = 1e-6

LANES = 128
HEAD_PAD = LANES
MXU_DIM = 256
POOL_HALO = 8
LAT_WIDTH = Q_LORA + KV_LORA + HEAD_PAD
REST_WIDTH = 2 * POOL_WIDTH + ATTN_WIDTH
VMEM_LIMIT = 56 * 1024 * 1024
ROPE_LO = QK_NOPE + QK_ROPE // 2


def _silu(x):
    return x * jax.nn.sigmoid(x)


def _rope_kernel(pos_ref, inv_ref, cos_ref, sin_ref):
    ang = pos_ref[...].astype(jnp.float32) * inv_ref[...]
    cos_ref[...] = jnp.cos(ang)
    sin_ref[...] = jnp.sin(ang)


def _rope_tables(positions):
    b, s = positions.shape
    half = QK_ROPE // 2
    inv = ROPE_THETA ** (-jnp.arange(0, QK_ROPE, 2, dtype=jnp.float32) / QK_ROPE)
    rows = b * s * half // LANES
    pos_rep = jnp.repeat(positions.reshape(-1), half).reshape(rows, LANES)
    inv_t = jnp.tile(inv, LANES // half).reshape(1, LANES)
    blk = rows // 8
    cos_c, sin_c = pl.pallas_call(
        _rope_kernel,
        grid=(rows // blk,),
        in_specs=[pl.BlockSpec((blk, LANES), lambda i: (i, 0)),
                  pl.BlockSpec((1, LANES), lambda i: (0, 0))],
        out_specs=[pl.BlockSpec((blk, LANES), lambda i: (i, 0)),
                   pl.BlockSpec((blk, LANES), lambda i: (i, 0))],
        out_shape=[jax.ShapeDtypeStruct((rows, LANES), jnp.float32)] * 2,
        name="rope_tables",
    )(pos_rep, inv_t)
    cos = cos_c.reshape(b, s, half)
    sin = sin_c.reshape(b, s, half)
    ones = jnp.ones((b, s, QK_NOPE), jnp.float32)
    tail1 = jnp.ones((b, s, HEAD_PAD - QK_HEAD), jnp.float32)
    zeros = jnp.zeros((b, s, QK_NOPE), jnp.float32)
    tail0 = jnp.zeros((b, s, HEAD_PAD - QK_HEAD), jnp.float32)
    cosf = jnp.concatenate([ones, cos, cos, tail1], axis=-1)
    sinf = jnp.concatenate([zeros, -sin, sin, tail0], axis=-1)
    return cosf, sinf


def _mod_kernel(c_ref, w_ref, b_ref, o_ref):
    c_act = _silu(c_ref[...])
    o_ref[...] = jnp.dot(c_act, w_ref[...], preferred_element_type=jnp.float32,
                         precision=lax.Precision.HIGHEST) + b_ref[...]


def _ada_mod(c, ada_w, ada_b):
    b, d = c.shape
    n = ada_w.shape[1]
    tn = 512
    return pl.pallas_call(
        _mod_kernel,
        grid=(n // tn,),
        in_specs=[pl.BlockSpec((b, d), lambda j: (0, 0)),
                  pl.BlockSpec((d, tn), lambda j: (0, j)),
                  pl.BlockSpec((1, tn), lambda j: (0, j))],
        out_specs=pl.BlockSpec((b, tn), lambda j: (0, j)),
        out_shape=jax.ShapeDtypeStruct((b, n), jnp.float32),
        name="ada_mod",
    )(c, ada_w, ada_b.reshape(1, n))


def _rms(x, n):
    return lax.rsqrt(jnp.sum(x * x, axis=-1, keepdims=True) * (1.0 / n) + EPS)


def _head_rms(xp, ones_ref):
    sq = (xp * xp).astype(jnp.bfloat16)
    parts = []
    for j in range(xp.shape[1] // MXU_DIM):
        sl = slice(j * MXU_DIM, (j + 1) * MXU_DIM)
        parts.append(jnp.dot(sq[:, sl], ones_ref[...], preferred_element_type=jnp.float32))
    ssq = jnp.concatenate(parts, axis=-1)
    return lax.rsqrt(ssq * (1.0 / QK_HEAD) + EPS)


def _proj_kernel(x_ref, mod_ref, ng_ref, wlat_ref, wrest_ref, gq_ref, wuq_ref, wuqr_ref, gkv_ref,
                 wuk_ref, wuvt_ref, gqn_ref, gqnr_ref, gkn_ref, ones_ref, cos_ref, sin_ref,
                 u_ref, gp_ref, q_ref, k_ref, vt_ref, ga_ref):
    x = x_ref[0]
    shift = mod_ref[0, 0:1, :]
    scale = mod_ref[0, 1:2, :]
    xn = (x * _rms(x, D_MODEL)) * (ng_ref[...] * (1.0 + scale)) + shift
    xn = xn.astype(jnp.bfloat16)
    lat = jnp.dot(xn, wlat_ref[...], preferred_element_type=jnp.float32)
    q_lat = lat[:, :Q_LORA]
    kv_lat = lat[:, Q_LORA:Q_LORA + KV_LORA]
    kr_blk = lat[:, Q_LORA + KV_LORA:]
    qn = (q_lat * _rms(q_lat, Q_LORA) * gq_ref[...]).astype(jnp.bfloat16)
    kvn = (kv_lat * _rms(kv_lat, KV_LORA) * gkv_ref[...]).astype(jnp.bfloat16)

    q_raw = jnp.dot(qn, wuq_ref[...], preferred_element_type=jnp.float32)
    q_rot = jnp.dot(qn, wuqr_ref[...], preferred_element_type=jnp.float32)
    k_nope = jnp.dot(kvn, wuk_ref[...], preferred_element_type=jnp.float32)
    vt = lax.dot_general(wuvt_ref[...], kvn, (((1,), (1,)), ((), ())),
                         preferred_element_type=jnp.float32)
    vt_ref[0] = vt.astype(jnp.bfloat16)

    rest = jnp.dot(xn, wrest_ref[...], preferred_element_type=jnp.float32)
    u_ref[0] = rest[:, :POOL_WIDTH]
    gp_ref[0] = rest[:, POOL_WIDTH:2 * POOL_WIDTH]
    ga_ref[0] = rest[:, 2 * POOL_WIDTH:]

    cosf = cos_ref[0]
    sinf = sin_ref[0]
    qc = QK_HEAD ** -0.5 * math.log2(math.e)
    q_cos = cosf * (gqn_ref[...] * qc)
    q_sin = sinf * (gqnr_ref[...] * qc)
    k_cos = cosf * gkn_ref[...]
    lane = lax.broadcasted_iota(jnp.int32, (1, HEAD_PAD), 1)
    kg = kr_blk * gkn_ref[...]
    k_rot = jnp.where(lane < ROPE_LO, pltpu.roll(kg, HEAD_PAD - QK_ROPE // 2, 1),
                      pltpu.roll(kg, QK_ROPE // 2, 1)) * sinf

    kc = k_nope + jnp.concatenate([kr_blk] * N_HEADS, axis=-1)
    rq = _head_rms(q_raw, ones_ref)
    rk = _head_rms(kc, ones_ref)
    for h in range(N_HEADS):
        sl = slice(h * HEAD_PAD, (h + 1) * HEAD_PAD)
        qh = rq[:, sl] * (q_raw[:, sl] * q_cos + q_rot[:, sl] * q_sin)
        q_ref[0, :, sl] = qh.astype(jnp.bfloat16)
        kh = rk[:, sl] * (kc[:, sl] * k_cos + k_rot)
        k_ref[0, :, sl] = kh.astype(jnp.bfloat16)


def _proj_stage(x, mod3, consts, cosf, sinf, tm):
    b, s, d = x.shape
    hp = N_HEADS * HEAD_PAD
    tok = lambda w: pl.BlockSpec((1, tm, w), lambda bi, i: (bi, i, 0))
    full = lambda a: pl.BlockSpec(a.shape, lambda bi, i: (0,) * a.ndim)
    return pl.pallas_call(
        _proj_kernel,
        grid=(b, s // tm),
        in_specs=[tok(d), pl.BlockSpec((1, 3, d), lambda bi, i: (bi, 0, 0))]
                 + [full(a) for a in consts] + [tok(HEAD_PAD), tok(HEAD_PAD)],
        out_specs=[tok(POOL_WIDTH), tok(POOL_WIDTH), tok(hp), tok(hp),
                   pl.BlockSpec((1, ATTN_WIDTH, tm), lambda bi, i: (bi, 0, i)), tok(ATTN_WIDTH)],
        out_shape=[jax.ShapeDtypeStruct((b, s, POOL_WIDTH), jnp.float32),
                   jax.ShapeDtypeStruct((b, s, POOL_WIDTH), jnp.float32),
                   jax.ShapeDtypeStruct((b, s, hp), jnp.bfloat16),
                   jax.ShapeDtypeStruct((b, s, hp), jnp.bfloat16),
                   jax.ShapeDtypeStruct((b, ATTN_WIDTH, s), jnp.bfloat16),
                   jax.ShapeDtypeStruct((b, s, ATTN_WIDTH), jnp.float32)],
        compiler_params=pltpu.CompilerParams(
            dimension_semantics=("arbitrary", "arbitrary"), vmem_limit_bytes=VMEM_LIMIT),
        name="proj_stage",
    )(x, mod3, *consts, cosf, sinf)


def _window_sum(u, w):
    acc = u + pltpu.roll(u, 1, 0)
    span = 2
    n = u.shape[0]
    while span < w:
        half = span // 2
        acc = pltpu.roll(acc, n - half, 0) + pltpu.roll(acc, half, 0)
        span *= 2
    return acc


def _pool_kernel(u_ref, g_ref, w_ref, sc_ref, o_ref):
    s = u_ref.shape[1]
    t = lax.broadcasted_iota(jnp.int32, (s, 1), 0)
    zpad = jnp.zeros((POOL_HALO, POOL_GROUP), jnp.float32)
    for gi, w in enumerate(POOL_WINDOWS):
        sl = slice(gi * POOL_GROUP, (gi + 1) * POOL_GROUP)
        u = u_ref[0, :, sl]
        up = jnp.concatenate([zpad, u, zpad], axis=0)
        wsum = _window_sum(up, w)[POOL_HALO:POOL_HALO + s]
        lo = jnp.maximum(t - w // 2, 0)
        hi = jnp.minimum(t + (w - w // 2), s)
        cnt = (hi - lo).astype(jnp.float32)
        diff = wsum / cnt - u
        y = jnp.dot(diff.astype(jnp.bfloat16), w_ref[gi], preferred_element_type=jnp.float32)
        y = y * sc_ref[:, sl] * _silu(g_ref[0, :, sl])
        o_ref[0, :, sl] = y.astype(jnp.bfloat16)


def _pool_stage(u_pool, g_pool, pool_w_bf, pool_scale):
    b, s, p = u_pool.shape
    seq = pl.BlockSpec((1, s, p), lambda bi: (bi, 0, 0))
    return pl.pallas_call(
        _pool_kernel,
        grid=(b,),
        in_specs=[seq, seq, pl.BlockSpec(pool_w_bf.shape, lambda bi: (0, 0, 0)),
                  pl.BlockSpec((1, p), lambda bi: (0, 0))],
        out_specs=seq,
        out_shape=jax.ShapeDtypeStruct((b, s, p), jnp.bfloat16),
        compiler_params=pltpu.CompilerParams(
            dimension_semantics=("arbitrary",), vmem_limit_bytes=VMEM_LIMIT),
        name="pool_stage",
    )(u_pool, g_pool, pool_w_bf, pool_scale)


def _attn_kernel(q_ref, k_ref, vt_ref, ga_ref, yp_ref, x_ref, mod_ref, wo_ref, o_ref):
    def scores_t(h):
        sl = slice(h * HEAD_PAD, (h + 1) * HEAD_PAD)
        kh = k_ref[0, :, sl]
        qh = q_ref[0, :, sl]
        return lax.dot_general(kh, qh, (((1,), (1,)), ((), ())),
                               preferred_element_type=jnp.float32)

    outs = []
    st_next = scores_t(0)
    for h in range(N_HEADS):
        st = st_next
        if h + 1 < N_HEADS:
            st_next = scores_t(h + 1)
        m = jnp.max(st, axis=0, keepdims=True)
        p = jnp.exp2(st - m)
        l = jnp.sum(p, axis=0, keepdims=True)
        ot = jnp.dot(vt_ref[0, h * V_HEAD:(h + 1) * V_HEAD, :], p.astype(jnp.bfloat16),
                     preferred_element_type=jnp.float32)
        outs.append(ot / l)
    o = jnp.concatenate(outs, axis=0).T
    ya = (o * _silu(ga_ref[0])).astype(jnp.bfloat16)
    ycat = jnp.concatenate([yp_ref[0], ya], axis=-1)
    y = jnp.dot(ycat, wo_ref[...], preferred_element_type=jnp.float32)
    gate = mod_ref[0, 2:3, :]
    o_ref[0] = x_ref[0] + gate * y


def _attn_stage(q, k, vt, g_attn, y_pool, x, mod3, w_out_bf, tq):
    b, s, d = x.shape
    hp = N_HEADS * HEAD_PAD
    tok = lambda w: pl.BlockSpec((1, tq, w), lambda bi, i: (bi, i, 0))
    return pl.pallas_call(
        _attn_kernel,
        grid=(b, s // tq),
        in_specs=[tok(hp), pl.BlockSpec((1, s, hp), lambda bi, i: (bi, 0, 0)),
                  pl.BlockSpec((1, ATTN_WIDTH, s), lambda bi, i: (bi, 0, 0)),
                  tok(ATTN_WIDTH), tok(POOL_WIDTH), tok(d),
                  pl.BlockSpec((1, 3, d), lambda bi, i: (bi, 0, 0)),
                  pl.BlockSpec(w_out_bf.shape, lambda bi, i: (0, 0))],
        out_specs=tok(d),
        out_shape=jax.ShapeDtypeStruct((b, s, d), jnp.float32),
        compiler_params=pltpu.CompilerParams(
            dimension_semantics=("arbitrary", "arbitrary"), vmem_limit_bytes=VMEM_LIMIT),
        name="attn_out_stage",
    )(q, k, vt, g_attn, y_pool, x, mod3, w_out_bf)


def _pad_heads(w, width):
    kdim = w.shape[0]
    w = w.reshape(kdim, N_HEADS, width)
    w = jnp.pad(w, ((0, 0), (0, 0), (0, HEAD_PAD - width)))
    return w.reshape(kdim, N_HEADS * HEAD_PAD)


def _rope_partner(a):
    half = QK_ROPE // 2
    return jnp.concatenate([jnp.zeros_like(a[..., :QK_NOPE]), a[..., QK_NOPE + half:],
                            a[..., QK_NOPE:QK_NOPE + half]], axis=-1)


def kernel(x, c, positions, ada_w, ada_b, norm_g, w_in, pool_w, pool_scale, g_q_lat, w_uq,
           g_kv_lat, w_ukv, g_qnorm, g_knorm, w_out):
    b, s, d = x.shape
    bf = jnp.bfloat16
    cosf, sinf = _rope_tables(positions)
    mod3 = _ada_mod(c, ada_w[0], ada_b[0]).reshape(b, 3, d)

    wi = w_in[0]
    o_q, o_kr, o_ga = 2 * POOL_WIDTH, sum(IN_SPLITS[:4]), sum(IN_SPLITS[:5])
    w_lat = jnp.concatenate(
        [wi[:, o_q:o_kr], jnp.zeros((d, QK_NOPE), wi.dtype), wi[:, o_kr:o_ga],
         jnp.zeros((d, HEAD_PAD - QK_HEAD), wi.dtype)], axis=1).astype(bf)
    w_rest = jnp.concatenate([wi[:, :o_q], wi[:, o_ga:]], axis=1).astype(bf)
    wq3 = w_uq[0].reshape(Q_LORA, N_HEADS, QK_HEAD)
    w_uq_p = _pad_heads(w_uq[0], QK_HEAD).astype(bf)
    w_uqr_p = _pad_heads(_rope_partner(wq3).reshape(Q_LORA, -1), QK_HEAD).astype(bf)
    wkv = w_ukv[0].reshape(KV_LORA, N_HEADS, QK_NOPE + V_HEAD)
    w_uk_p = _pad_heads(wkv[..., :QK_NOPE].reshape(KV_LORA, -1), QK_NOPE).astype(bf)
    w_uvt = wkv[..., QK_NOPE:].reshape(KV_LORA, ATTN_WIDTH).T.astype(bf)
    pad_g = lambda g: jnp.pad(g, (0, HEAD_PAD - QK_HEAD)).reshape(1, HEAD_PAD)
    lane_head = jnp.arange(MXU_DIM) // HEAD_PAD
    ones_bd = (lane_head[:, None] == lane_head[None, :]).astype(bf)
    consts = (norm_g[0].reshape(1, d), w_lat, w_rest, g_q_lat[0].reshape(1, Q_LORA), w_uq_p,
              w_uqr_p, g_kv_lat[0].reshape(1, KV_LORA), w_uk_p, w_uvt, pad_g(g_qnorm[0]),
              pad_g(_rope_partner(g_qnorm[0])), pad_g(g_knorm[0]), ones_bd)

    u_pool, g_pool, q, k, vt, g_attn = _proj_stage(x, mod3, consts, cosf, sinf, tm=512)
    y_pool = _pool_stage(u_pool, g_pool, pool_w[0].astype(bf), pool_scale[0].reshape(1, POOL_WIDTH))
    return _attn_stage(q, k, vt, g_attn, y_pool, x, mod3, w_out[0].astype(bf), tq=256)
```

```python
import math

import jax
import jax.numpy as jnp
from jax import lax
from jax.experimental import pallas as pl
from jax.experimental.pallas import tpu as pltpu

D_MODEL = 1024
POOL_WINDOWS = (2, 4, 8, 16)
POOL_WIDTH = 512
POOL_GROUP = 128
N_HEADS = 8
QK_NOPE = 64
QK_ROPE = 32
QK_HEAD = 96
V_HEAD = 64
ATTN_WIDTH = 512
Q_LORA = 256
KV_LORA = 128
IN_SPLITS = (512, 512, 256, 128, 32, 512)
ROPE_THETA = 10000.0
EPS = 1e-6

LANES = 128
HEAD_PAD = LANES
MXU_DIM = 256
POOL_HALO = 8
LAT_WIDTH = Q_LORA + KV_LORA + HEAD_PAD
REST_WIDTH = 2 * POOL_WIDTH + ATTN_WIDTH
VMEM_LIMIT = 56 * 1024 * 1024
ROPE_LO = QK_NOPE + QK_ROPE // 2
VT_HEAD = V_HEAD + 16
SCORE_LIMIT = 64.0


def _silu(x):
    return x * jax.nn.sigmoid(x)


def _rope_kernel(pos_ref, inv_ref, cos_ref, sin_ref):
    ang = pos_ref[0].astype(jnp.float32) * inv_ref[...]
    cos_ref[0] = jnp.cos(ang)
    sin_ref[0] = jnp.sin(ang)


def _rope_tables(positions):
    b, s = positions.shape
    half = QK_ROPE // 2
    inv = ROPE_THETA ** (-jnp.arange(0, QK_ROPE, 2, dtype=jnp.float32) / QK_ROPE)
    tab = pl.BlockSpec((1, half, s), lambda i: (i, 0, 0))
    return pl.pallas_call(
        _rope_kernel,
        grid=(b,),
        in_specs=[pl.BlockSpec((1, 1, s), lambda i: (i, 0, 0)),
                  pl.BlockSpec((half, 1), lambda i: (0, 0))],
        out_specs=[tab, tab],
        out_shape=[jax.ShapeDtypeStruct((b, half, s), jnp.float32)] * 2,
        name="rope_tables",
    )(positions.reshape(b, 1, s), inv.reshape(half, 1))


def _head_tables(cos_t, sin_t):
    tm = cos_t.shape[1]
    ones = lambda n: jnp.ones((n, tm), jnp.float32)
    zeros = lambda n: jnp.zeros((n, tm), jnp.float32)
    cosf = jnp.concatenate([ones(QK_NOPE), cos_t, cos_t, ones(HEAD_PAD - QK_HEAD)], axis=0)
    sinf = jnp.concatenate([zeros(QK_NOPE), -sin_t, sin_t, zeros(HEAD_PAD - QK_HEAD)], axis=0)
    return cosf.T, sinf.T


def _mod_kernel(c_ref, w_ref, b_ref, o_ref):
    c_act = _silu(c_ref[...])
    o_ref[...] = jnp.dot(c_act, w_ref[...], preferred_element_type=jnp.float32,
                         precision=lax.Precision.HIGHEST) + b_ref[...]


def _ada_mod(c, ada_w, ada_b):
    b, d = c.shape
    n = ada_w.shape[1]
    tn = 512
    return pl.pallas_call(
        _mod_kernel,
        grid=(n // tn,),
        in_specs=[pl.BlockSpec((b, d), lambda j: (0, 0)),
                  pl.BlockSpec((d, tn), lambda j: (0, j)),
                  pl.BlockSpec((1, tn), lambda j: (0, j))],
        out_specs=pl.BlockSpec((b, tn), lambda j: (0, j)),
        out_shape=jax.ShapeDtypeStruct((b, n), jnp.float32),
        name="ada_mod",
    )(c, ada_w, ada_b.reshape(1, n))


def _rms(x, n):
    return lax.rsqrt(jnp.sum(x * x, axis=-1, keepdims=True) * (1.0 / n) + EPS)


def _head_rms(xp, ones_ref):
    sq = (xp * xp).astype(jnp.bfloat16)
    parts = []
    for j in range(xp.shape[1] // MXU_DIM):
        sl = slice(j * MXU_DIM, (j + 1) * MXU_DIM)
        parts.append(jnp.dot(sq[:, sl], ones_ref[...], preferred_element_type=jnp.float32))
    ssq = jnp.concatenate(parts, axis=-1)
    return lax.rsqrt(ssq * (1.0 / QK_HEAD) + EPS)


def _proj_kernel(x_ref, mod_ref, ng_ref, wlat_ref, wrest_ref, gq_ref, wuq_ref, wuqr_ref, gkv_ref,
                 wuk_ref, wuvt_ref, gqn_ref, gqnr_ref, gkn_ref, ones_ref, cos_ref, sin_ref,
                 u_ref, gp_ref, q_ref, k_ref, vt_ref, ga_ref):
    x = x_ref[0]
    shift = mod_ref[0, 0:1, :]
    scale = mod_ref[0, 1:2, :]
    xn = (x * _rms(x, D_MODEL)) * (ng_ref[...] * (1.0 + scale)) + shift
    xn = xn.astype(jnp.bfloat16)
    lat = jnp.dot(xn, wlat_ref[...], preferred_element_type=jnp.float32)
    q_lat = lat[:, :Q_LORA]
    kv_lat = lat[:, Q_LORA:Q_LORA + KV_LORA]
    kr_blk = lat[:, Q_LORA + KV_LORA:]
    qn = (q_lat * _rms(q_lat, Q_LORA) * gq_ref[...]).astype(jnp.bfloat16)
    kvn = (kv_lat * _rms(kv_lat, KV_LORA) * gkv_ref[...]).astype(jnp.bfloat16)

    q_raw = jnp.dot(qn, wuq_ref[...], preferred_element_type=jnp.float32)
    q_rot = jnp.dot(qn, wuqr_ref[...], preferred_element_type=jnp.float32)
    k_nope = jnp.dot(kvn, wuk_ref[...], preferred_element_type=jnp.float32)
    vt = lax.dot_general(wuvt_ref[...], kvn, (((1,), (1,)), ((), ())),
                         preferred_element_type=jnp.float32)
    tm = x.shape[0]
    sum_rows = (lax.broadcasted_iota(jnp.int32, (VT_HEAD - V_HEAD, tm), 0) == 0)
    for h in range(N_HEADS):
        vt_ref[0, h * VT_HEAD:h * VT_HEAD + V_HEAD, :] = (
            vt[h * V_HEAD:(h + 1) * V_HEAD].astype(jnp.bfloat16))
        vt_ref[0, h * VT_HEAD + V_HEAD:(h + 1) * VT_HEAD, :] = sum_rows.astype(jnp.bfloat16)

    rest = jnp.dot(xn, wrest_ref[...], preferred_element_type=jnp.float32)
    u_ref[0] = rest[:, :POOL_WIDTH]
    gp_ref[0] = rest[:, POOL_WIDTH:2 * POOL_WIDTH]
    ga_ref[0] = rest[:, 2 * POOL_WIDTH:]

    cosf, sinf = _head_tables(cos_ref[0], sin_ref[0])
    qc = QK_HEAD ** -0.5 * math.log2(math.e)
    q_cos = cosf * (gqn_ref[...] * qc)
    q_sin = sinf * (gqnr_ref[...] * qc)
    k_cos = cosf * gkn_ref[...]
    lane = lax.broadcasted_iota(jnp.int32, (1, HEAD_PAD), 1)
    kg = kr_blk * gkn_ref[...]
    k_rot = jnp.where(lane < ROPE_LO, pltpu.roll(kg, HEAD_PAD - QK_ROPE // 2, 1),
                      pltpu.roll(kg, QK_ROPE // 2, 1)) * sinf

    kc = k_nope + jnp.concatenate([kr_blk] * N_HEADS, axis=-1)
    rq = _head_rms(q_raw, ones_ref)
    rk = _head_rms(kc, ones_ref)
    for h in range(N_HEADS):
        sl = slice(h * HEAD_PAD, (h + 1) * HEAD_PAD)
        qh = rq[:, sl] * (q_raw[:, sl] * q_cos + q_rot[:, sl] * q_sin)
        q_ref[0, :, sl] = qh.astype(jnp.bfloat16)
        kh = rk[:, sl] * (kc[:, sl] * k_cos + k_rot)
        k_ref[0, :, sl] = kh.astype(jnp.bfloat16)


def _proj_stage(x, mod3, consts, cos_t, sin_t, tm):
    b, s, d = x.shape
    hp = N_HEADS * HEAD_PAD
    tok = lambda w: pl.BlockSpec((1, tm, w), lambda bi, i: (bi, i, 0))
    full = lambda a: pl.BlockSpec(a.shape, lambda bi, i: (0,) * a.ndim)
    tab = pl.BlockSpec((1, QK_ROPE // 2, tm), lambda bi, i: (bi, 0, i))
    return pl.pallas_call(
        _proj_kernel,
        grid=(b, s // tm),
        in_specs=[tok(d), pl.BlockSpec((1, 3, d), lambda bi, i: (bi, 0, 0))]
                 + [full(a) for a in consts] + [tab, tab],
        out_specs=[tok(POOL_WIDTH), tok(POOL_WIDTH), tok(hp), tok(hp),
                   pl.BlockSpec((1, N_HEADS * VT_HEAD, tm), lambda bi, i: (bi, 0, i)), tok(ATTN_WIDTH)],
        out_shape=[jax.ShapeDtypeStruct((b, s, POOL_WIDTH), jnp.float32),
                   jax.ShapeDtypeStruct((b, s, POOL_WIDTH), jnp.float32),
                   jax.ShapeDtypeStruct((b, s, hp), jnp.bfloat16),
                   jax.ShapeDtypeStruct((b, s, hp), jnp.bfloat16),
                   jax.ShapeDtypeStruct((b, N_HEADS * VT_HEAD, s), jnp.bfloat16),
                   jax.ShapeDtypeStruct((b, s, ATTN_WIDTH), jnp.float32)],
        compiler_params=pltpu.CompilerParams(
            dimension_semantics=("arbitrary", "arbitrary"), vmem_limit_bytes=VMEM_LIMIT),
        name="proj_stage",
    )(x, mod3, *consts, cos_t, sin_t)


def _window_sum(u, w):
    acc = u + pltpu.roll(u, 1, 0)
    span = 2
    n = u.shape[0]
    while span < w:
        half = span // 2
        acc = pltpu.roll(acc, n - half, 0) + pltpu.roll(acc, half, 0)
        span *= 2
    return acc


def _pool_kernel(u_ref, g_ref, w_ref, sc_ref, o_ref):
    s = u_ref.shape[1]
    t = lax.broadcasted_iota(jnp.int32, (s, 1), 0)
    zpad = jnp.zeros((POOL_HALO, POOL_GROUP), jnp.float32)
    for gi, w in enumerate(POOL_WINDOWS):
        sl = slice(gi * POOL_GROUP, (gi + 1) * POOL_GROUP)
        u = u_ref[0, :, sl]
        up = jnp.concatenate([zpad, u, zpad], axis=0)
        wsum = _window_sum(up, w)[POOL_HALO:POOL_HALO + s]
        lo = jnp.maximum(t - w // 2, 0)
        hi = jnp.minimum(t + (w - w // 2), s)
        cnt = (hi - lo).astype(jnp.float32)
        diff = wsum / cnt - u
        y = jnp.dot(diff.astype(jnp.bfloat16), w_ref[gi], preferred_element_type=jnp.float32)
        y = y * sc_ref[:, sl] * _silu(g_ref[0, :, sl])
        o_ref[0, :, sl] = y.astype(jnp.bfloat16)


def _pool_stage(u_pool, g_pool, pool_w_bf, pool_scale):
    b, s, p = u_pool.shape
    seq = pl.BlockSpec((1, s, p), lambda bi: (bi, 0, 0))
    return pl.pallas_call(
        _pool_kernel,
        grid=(b,),
        in_specs=[seq, seq, pl.BlockSpec(pool_w_bf.shape, lambda bi: (0, 0, 0)),
                  pl.BlockSpec((1, p), lambda bi: (0, 0))],
        out_specs=seq,
        out_shape=jax.ShapeDtypeStruct((b, s, p), jnp.bfloat16),
        compiler_params=pltpu.CompilerParams(
            dimension_semantics=("arbitrary",), vmem_limit_bytes=VMEM_LIMIT),
        name="pool_stage",
    )(u_pool, g_pool, pool_w_bf, pool_scale)


def _attn_body(shift_max, q_ref, k_ref, vt_ref, ga_ref, yp_ref, x_ref, mod_ref, wo_ref, o_ref):
    def scores_t(h):
        sl = slice(h * HEAD_PAD, (h + 1) * HEAD_PAD)
        kh = k_ref[0, :, sl]
        qh = q_ref[0, :, sl]
        return lax.dot_general(kh, qh, (((1,), (1,)), ((), ())),
                               preferred_element_type=jnp.float32)

    outs = []
    st_next = scores_t(0)
    for h in range(N_HEADS):
        st = st_next
        if h + 1 < N_HEADS:
            st_next = scores_t(h + 1)
        if shift_max:
            st = st - jnp.max(st, axis=0, keepdims=True)
        p = jnp.exp2(st).astype(jnp.bfloat16)
        ot = jnp.dot(vt_ref[0, h * VT_HEAD:(h + 1) * VT_HEAD, :], p,
                     preferred_element_type=jnp.float32)
        outs.append(ot[:V_HEAD] / ot[V_HEAD:V_HEAD + 1])
    o = jnp.concatenate(outs, axis=0).T
    ya = (o * _silu(ga_ref[0])).astype(jnp.bfloat16)
    ycat = jnp.concatenate([yp_ref[0], ya], axis=-1)
    y = jnp.dot(ycat, wo_ref[...], preferred_element_type=jnp.float32)
    gate = mod_ref[0, 2:3, :]
    o_ref[0] = x_ref[0] + gate * y


def _attn_kernel(bounded_ref, *refs):
    @pl.when(bounded_ref[0] != 0)
    def _():
        _attn_body(False, *refs)

    @pl.when(bounded_ref[0] == 0)
    def _():
        _attn_body(True, *refs)


def _attn_stage(bounded, q, k, vt, g_attn, y_pool, x, mod3, w_out_bf, tq):
    b, s, d = x.shape
    hp = N_HEADS * HEAD_PAD
    tok = lambda w: pl.BlockSpec((1, tq, w), lambda bi, i: (bi, i, 0))
    return pl.pallas_call(
        _attn_kernel,
        grid=(b, s // tq),
        in_specs=[pl.BlockSpec(memory_space=pltpu.SMEM),
                  tok(hp), pl.BlockSpec((1, s, hp), lambda bi, i: (bi, 0, 0)),
                  pl.BlockSpec((1, N_HEADS * VT_HEAD, s), lambda bi, i: (bi, 0, 0)),
                  tok(ATTN_WIDTH), tok(POOL_WIDTH), tok(d),
                  pl.BlockSpec((1, 3, d), lambda bi, i: (bi, 0, 0)),
                  pl.BlockSpec(w_out_bf.shape, lambda bi, i: (0, 0))],
        out_specs=tok(d),
        out_shape=jax.ShapeDtypeStruct((b, s, d), jnp.float32),
        compiler_params=pltpu.CompilerParams(
            dimension_semantics=("arbitrary", "arbitrary"), vmem_limit_bytes=VMEM_LIMIT),
        name="attn_out_stage",
    )(bounded, q, k, vt, g_attn, y_pool, x, mod3, w_out_bf)


def _pad_heads(w, width):
    kdim = w.shape[0]
    w = w.reshape(kdim, N_HEADS, width)
    w = jnp.pad(w, ((0, 0), (0, 0), (0, HEAD_PAD - width)))
    return w.reshape(kdim, N_HEADS * HEAD_PAD)


def _rope_partner(a):
    half = QK_ROPE // 2
    return jnp.concatenate([jnp.zeros_like(a[..., :QK_NOPE]), a[..., QK_NOPE + half:],
                            a[..., QK_NOPE:QK_NOPE + half]], axis=-1)


def kernel(x, c, positions, ada_w, ada_b, norm_g, w_in, pool_w, pool_scale, g_q_lat, w_uq,
           g_kv_lat, w_ukv, g_qnorm, g_knorm, w_out):
    b, s, d = x.shape
    bf = jnp.bfloat16
    cos_t, sin_t = _rope_tables(positions)
    mod3 = _ada_mod(c, ada_w[0], ada_b[0]).reshape(b, 3, d)

    wi = w_in[0]
    o_q, o_kr, o_ga = 2 * POOL_WIDTH, sum(IN_SPLITS[:4]), sum(IN_SPLITS[:5])
    w_lat = jnp.concatenate(
        [wi[:, o_q:o_kr], jnp.zeros((d, QK_NOPE), wi.dtype), wi[:, o_kr:o_ga],
         jnp.zeros((d, HEAD_PAD - QK_HEAD), wi.dtype)], axis=1).astype(bf)
    w_rest = jnp.concatenate([wi[:, :o_q], wi[:, o_ga:]], axis=1).astype(bf)
    wq3 = w_uq[0].reshape(Q_LORA, N_HEADS, QK_HEAD)
    w_uq_p = _pad_heads(w_uq[0], QK_HEAD).astype(bf)
    w_uqr_p = _pad_heads(_rope_partner(wq3).reshape(Q_LORA, -1), QK_HEAD).astype(bf)
    wkv = w_ukv[0].reshape(KV_LORA, N_HEADS, QK_NOPE + V_HEAD)
    w_uk_p = _pad_heads(wkv[..., :QK_NOPE].reshape(KV_LORA, -1), QK_NOPE).astype(bf)
    w_uvt = wkv[..., QK_NOPE:].reshape(KV_LORA, ATTN_WIDTH).T.astype(bf)
    pad_g = lambda g: jnp.pad(g, (0, HEAD_PAD - QK_HEAD)).reshape(1, HEAD_PAD)
    lane_head = jnp.arange(MXU_DIM) // HEAD_PAD
    ones_bd = (lane_head[:, None] == lane_head[None, :]).astype(bf)
    consts = (norm_g[0].reshape(1, d), w_lat, w_rest, g_q_lat[0].reshape(1, Q_LORA), w_uq_p,
              w_uqr_p, g_kv_lat[0].reshape(1, KV_LORA), w_uk_p, w_uvt, pad_g(g_qnorm[0]),
              pad_g(_rope_partner(g_qnorm[0])), pad_g(g_knorm[0]), ones_bd)

    u_pool, g_pool, q, k, vt, g_attn = _proj_stage(x, mod3, consts, cos_t, sin_t, tm=512)
    y_pool = _pool_stage(u_pool, g_pool, pool_w[0].astype(bf), pool_scale[0].reshape(1, POOL_WIDTH))
    score_bound = (math.log2(math.e) * QK_HEAD ** 0.5
                   * jnp.max(jnp.abs(g_qnorm[0])) * jnp.max(jnp.abs(g_knorm[0])))
    bounded = (score_bound <= SCORE_LIMIT).astype(jnp.int32).reshape(1)
    return _attn_stage(bounded, q, k, vt, g_attn, y_pool, x, mod3, w_out[0].astype(bf), tq=256)
```

```python
import math

import jax
import jax.numpy as jnp
from jax import lax
from jax.experimental import pallas as pl
from jax.experimental.pallas import tpu as pltpu

D_MODEL = 1024
POOL_WINDOWS = (2, 4, 8, 16)
POOL_WIDTH = 512
POOL_GROUP = 128
N_HEADS = 8
QK_NOPE = 64
QK_ROPE = 32
QK_HEAD = 96
V_HEAD = 64
ATTN_WIDTH = 512
Q_LORA = 256
KV_LORA = 128
IN_SPLITS = (512, 512, 256, 128, 32, 512)
ROPE_THETA = 10000.0
EPS = 1e-6

LANES = 128
SUBLANES = 8
HEAD_PAD = LANES
MXU_DIM = 256
POOL_HALO = max(POOL_WINDOWS) // 2
VMEM_LIMIT = 56 * 1024 * 1024
ROPE_LO = QK_NOPE + QK_ROPE // 2
SCORE_LIMIT = 64.0
PROJ_TILE = 512
ATTN_TILE = 512

assert POOL_HALO == SUBLANES


def _silu(x):
    return x * jax.nn.sigmoid(x)


def _rope_kernel(pos_ref, inv_ref, cos_ref, sin_ref):
    ang = pos_ref[0].astype(jnp.float32) * inv_ref[...]
    cos_ref[0] = jnp.cos(ang)
    sin_ref[0] = jnp.sin(ang)


def _rope_tables(positions):
    b, s = positions.shape
    half = QK_ROPE // 2
    inv = ROPE_THETA ** (-jnp.arange(0, QK_ROPE, 2, dtype=jnp.float32) / QK_ROPE)
    tab = pl.BlockSpec((1, half, s), lambda i: (i, 0, 0))
    return pl.pallas_call(
        _rope_kernel,
        grid=(b,),
        in_specs=[pl.BlockSpec((1, 1, s), lambda i: (i, 0, 0)),
                  pl.BlockSpec((half, 1), lambda i: (0, 0))],
        out_specs=[tab, tab],
        out_shape=[jax.ShapeDtypeStruct((b, half, s), jnp.float32)] * 2,
        name="rope_tables",
    )(positions.reshape(b, 1, s), inv.reshape(half, 1))


def _head_tables(cos_t, sin_t):
    tm = cos_t.shape[1]
    ones = lambda n: jnp.ones((n, tm), jnp.float32)
    zeros = lambda n: jnp.zeros((n, tm), jnp.float32)
    cosf = jnp.concatenate([ones(QK_NOPE), cos_t, cos_t, ones(HEAD_PAD - QK_HEAD)], axis=0)
    sinf = jnp.concatenate([zeros(QK_NOPE), -sin_t, sin_t, zeros(HEAD_PAD - QK_HEAD)], axis=0)
    return cosf.T, sinf.T


def _mod_kernel(c_ref, w_ref, b_ref, o_ref):
    c_act = _silu(c_ref[...])
    o_ref[...] = jnp.dot(c_act, w_ref[...], preferred_element_type=jnp.float32,
                         precision=lax.Precision.HIGHEST) + b_ref[...]


def _ada_mod(c, ada_w, ada_b):
    b, d = c.shape
    n = ada_w.shape[1]
    tn = 512
    return pl.pallas_call(
        _mod_kernel,
        grid=(n // tn,),
        in_specs=[pl.BlockSpec((b, d), lambda j: (0, 0)),
                  pl.BlockSpec((d, tn), lambda j: (0, j)),
                  pl.BlockSpec((1, tn), lambda j: (0, j))],
        out_specs=pl.BlockSpec((b, tn), lambda j: (0, j)),
        out_shape=jax.ShapeDtypeStruct((b, n), jnp.float32),
        name="ada_mod",
    )(c, ada_w, ada_b.reshape(1, n))


def _rms(x, n):
    return lax.rsqrt(jnp.sum(x * x, axis=-1, keepdims=True) * (1.0 / n) + EPS)


def _head_rms(xp, ones_ref):
    sq = (xp * xp).astype(jnp.bfloat16)
    parts = []
    for j in range(xp.shape[1] // MXU_DIM):
        sl = slice(j * MXU_DIM, (j + 1) * MXU_DIM)
        parts.append(jnp.dot(sq[:, sl], ones_ref[...], preferred_element_type=jnp.float32))
    ssq = jnp.concatenate(parts, axis=-1)
    return lax.rsqrt(ssq * (1.0 / QK_HEAD) + EPS)


def _window_sum(u, w):
    acc = u + pltpu.roll(u, 1, 0)
    span = 2
    n = u.shape[0]
    while span < w:
        half = span // 2
        acc = pltpu.roll(acc, n - half, 0) + pltpu.roll(acc, half, 0)
        span *= 2
    return acc


def _pool_branch(u, u_halo, g, t0, seq_len, wp_ref, ps_ref):
    tm = u.shape[0]
    t = t0 + lax.broadcasted_iota(jnp.int32, (tm, 1), 0)
    u_ext = jnp.concatenate([u_halo[:POOL_HALO], u, u_halo[POOL_HALO:]], axis=0)
    diffs = []
    for gi, w in enumerate(POOL_WINDOWS):
        sl = slice(gi * POOL_GROUP, (gi + 1) * POOL_GROUP)
        wsum = _window_sum(u_ext[:, sl], w)[POOL_HALO:POOL_HALO + tm]
        cnt = jnp.minimum(t + (w - w // 2), seq_len) - jnp.maximum(t - w // 2, 0)
        diffs.append((wsum / cnt.astype(jnp.float32) - u[:, sl]).astype(jnp.bfloat16))
    ys = []
    for j in range(POOL_WIDTH // MXU_DIM):
        d2 = jnp.concatenate(diffs[2 * j:2 * j + 2], axis=-1)
        ys.append(jnp.dot(d2, wp_ref[j], preferred_element_type=jnp.float32))
    y = jnp.concatenate(ys, axis=-1)
    return y * ps_ref[...] * _silu(g)


def _proj_kernel(x_ref, xp_ref, xn_ref, mod_ref, ng_ref, wlat_ref, wrest_ref, gq_ref, wuq_ref,
                 wuqr_ref, gkv_ref, wuk_ref, wuvt_ref, gqn_ref, gqnr_ref, gkn_ref, ones_ref,
                 wp_ref, ps_ref, cos_ref, sin_ref,
                 yp_ref, q_ref, k_ref, vt_ref, ga_ref):
    i = pl.program_id(1)
    tm = x_ref.shape[1]
    seq_len = tm * pl.num_programs(1)
    shift = mod_ref[0, 0:1, :]
    gain = ng_ref[...] * (1.0 + mod_ref[0, 1:2, :])

    def norm_mod(x):
        return ((x * _rms(x, D_MODEL)) * gain + shift).astype(jnp.bfloat16)

    xn = norm_mod(x_ref[0])
    lat = jnp.dot(xn, wlat_ref[...], preferred_element_type=jnp.float32)
    q_lat = lat[:, :Q_LORA]
    kv_lat = lat[:, Q_LORA:Q_LORA + KV_LORA]
    kr_blk = lat[:, Q_LORA + KV_LORA:]
    qn = (q_lat * _rms(q_lat, Q_LORA) * gq_ref[...]).astype(jnp.bfloat16)
    kvn = (kv_lat * _rms(kv_lat, KV_LORA) * gkv_ref[...]).astype(jnp.bfloat16)

    q_raw = jnp.dot(qn, wuq_ref[...], preferred_element_type=jnp.float32)
    q_rot = jnp.dot(qn, wuqr_ref[...], preferred_element_type=jnp.float32)
    k_nope = jnp.dot(kvn, wuk_ref[...], preferred_element_type=jnp.float32)
    vt = lax.dot_general(wuvt_ref[...], kvn, (((1,), (1,)), ((), ())),
                         preferred_element_type=jnp.float32)
    vt_ref[0] = vt.astype(jnp.bfloat16)

    rest = jnp.dot(xn, wrest_ref[...], preferred_element_type=jnp.float32)
    ga_ref[0] = rest[:, 2 * POOL_WIDTH:]

    xh = norm_mod(jnp.concatenate([xp_ref[0], xn_ref[0]], axis=0))
    u_halo = jnp.dot(xh, wrest_ref[:, :POOL_WIDTH], preferred_element_type=jnp.float32)
    row = lax.broadcasted_iota(jnp.int32, (2 * POOL_HALO, 1), 0)
    has_prev = (i > 0).astype(jnp.float32)
    has_next = (i < pl.num_programs(1) - 1).astype(jnp.float32)
    u_halo = u_halo * jnp.where(row < POOL_HALO, has_prev, has_next)
    yp = _pool_branch(rest[:, :POOL_WIDTH], u_halo, rest[:, POOL_WIDTH:2 * POOL_WIDTH],
                      i * tm, seq_len, wp_ref, ps_ref)
    yp_ref[0] = yp.astype(jnp.bfloat16)

    cosf, sinf = _head_tables(cos_ref[0], sin_ref[0])
    qc = QK_HEAD ** -0.5 * math.log2(math.e)
    q_cos = cosf * (gqn_ref[...] * qc)
    q_sin = sinf * (gqnr_ref[...] * qc)
    k_cos = cosf * gkn_ref[...]
    lane = lax.broadcasted_iota(jnp.int32, (1, HEAD_PAD), 1)
    kg = kr_blk * gkn_ref[...]
    k_rot = jnp.where(lane < ROPE_LO, pltpu.roll(kg, HEAD_PAD - QK_ROPE // 2, 1),
                      pltpu.roll(kg, QK_ROPE // 2, 1)) * sinf

    kc = k_nope + jnp.concatenate([kr_blk] * N_HEADS, axis=-1)
    rq = _head_rms(q_raw, ones_ref)
    rk = _head_rms(kc, ones_ref)
    for h in range(N_HEADS):
        sl = slice(h * HEAD_PAD, (h + 1) * HEAD_PAD)
        qh = rq[:, sl] * (q_raw[:, sl] * q_cos + q_rot[:, sl] * q_sin)
        q_ref[0, :, sl] = qh.astype(jnp.bfloat16)
        kh = rk[:, sl] * (kc[:, sl] * k_cos + k_rot)
        k_ref[0, :, sl] = kh.astype(jnp.bfloat16)


def _proj_stage(x, mod3, consts, cos_t, sin_t):
    b, s, d = x.shape
    tm = PROJ_TILE
    hp = N_HEADS * HEAD_PAD
    halo_blocks = tm // POOL_HALO
    last_block = s // POOL_HALO - 1
    tok = lambda w: pl.BlockSpec((1, tm, w), lambda bi, i: (bi, i, 0))
    full = lambda a: pl.BlockSpec(a.shape, lambda bi, i: (0,) * a.ndim)
    tab = pl.BlockSpec((1, QK_ROPE // 2, tm), lambda bi, i: (bi, 0, i))
    prev = pl.BlockSpec((1, POOL_HALO, d),
                        lambda bi, i: (bi, jnp.maximum(i * halo_blocks - 1, 0), 0))
    nxt = pl.BlockSpec((1, POOL_HALO, d),
                       lambda bi, i: (bi, jnp.minimum((i + 1) * halo_blocks, last_block), 0))
    return pl.pallas_call(
        _proj_kernel,
        grid=(b, s // tm),
        in_specs=[tok(d), prev, nxt, pl.BlockSpec((1, 3, d), lambda bi, i: (bi, 0, 0))]
                 + [full(a) for a in consts] + [tab, tab],
        out_specs=[tok(POOL_WIDTH), tok(hp), tok(hp),
                   pl.BlockSpec((1, ATTN_WIDTH, tm), lambda bi, i: (bi, 0, i)), tok(ATTN_WIDTH)],
        out_shape=[jax.ShapeDtypeStruct((b, s, POOL_WIDTH), jnp.bfloat16),
                   jax.ShapeDtypeStruct((b, s, hp), jnp.bfloat16),
                   jax.ShapeDtypeStruct((b, s, hp), jnp.bfloat16),
                   jax.ShapeDtypeStruct((b, ATTN_WIDTH, s), jnp.bfloat16),
                   jax.ShapeDtypeStruct((b, s, ATTN_WIDTH), jnp.float32)],
        compiler_params=pltpu.CompilerParams(
            dimension_semantics=("arbitrary", "arbitrary"), vmem_limit_bytes=VMEM_LIMIT),
        name="proj_stage",
    )(x, x, x, mod3, *consts, cos_t, sin_t)


def _attn_body(shift_max, q_ref, k_ref, vt_ref, ga_ref, yp_ref, x_ref, mod_ref, wo_ref, o_ref):
    def scores_t(h):
        sl = slice(h * HEAD_PAD, (h + 1) * HEAD_PAD)
        kh = k_ref[0, :, sl]
        qh = q_ref[0, :, sl]
        return lax.dot_general(kh, qh, (((1,), (1,)), ((), ())),
                               preferred_element_type=jnp.float32)

    outs = []
    st_next = scores_t(0)
    for h in range(N_HEADS):
        st = st_next
        if h + 1 < N_HEADS:
            st_next = scores_t(h + 1)
        if shift_max:
            st = st - jnp.max(st, axis=0, keepdims=True)
        p = jnp.exp2(st)
        l = jnp.sum(p, axis=0, keepdims=True)
        ot = jnp.dot(vt_ref[0, h * V_HEAD:(h + 1) * V_HEAD, :], p.astype(jnp.bfloat16),
                     preferred_element_type=jnp.float32)
        outs.append(ot / l)
    o = jnp.concatenate(outs, axis=0).T
    ya = (o * _silu(ga_ref[0])).astype(jnp.bfloat16)
    ycat = jnp.concatenate([yp_ref[0], ya], axis=-1)
    y = jnp.dot(ycat, wo_ref[...], preferred_element_type=jnp.float32)
    gate = mod_ref[0, 2:3, :]
    o_ref[0] = x_ref[0] + gate * y


def _attn_kernel(bounded_ref, *refs):
    @pl.when(bounded_ref[0] != 0)
    def _():
        _attn_body(False, *refs)

    @pl.when(bounded_ref[0] == 0)
    def _():
        _attn_body(True, *refs)


def _attn_stage(bounded, q, k, vt, g_attn, y_pool, x, mod3, w_out_bf):
    b, s, d = x.shape
    tq = ATTN_TILE
    hp = N_HEADS * HEAD_PAD
    tok = lambda w: pl.BlockSpec((1, tq, w), lambda bi, i: (bi, i, 0))
    return pl.pallas_call(
        _attn_kernel,
        grid=(b, s // tq),
        in_specs=[pl.BlockSpec(memory_space=pltpu.SMEM),
                  tok(hp), pl.BlockSpec((1, s, hp), lambda bi, i: (bi, 0, 0)),
                  pl.BlockSpec((1, ATTN_WIDTH, s), lambda bi, i: (bi, 0, 0)),
                  tok(ATTN_WIDTH), tok(POOL_WIDTH), tok(d),
                  pl.BlockSpec((1, 3, d), lambda bi, i: (bi, 0, 0)),
                  pl.BlockSpec(w_out_bf.shape, lambda bi, i: (0, 0))],
        out_specs=tok(d),
        out_shape=jax.ShapeDtypeStruct((b, s, d), jnp.float32),
        compiler_params=pltpu.CompilerParams(
            dimension_semantics=("arbitrary", "arbitrary"), vmem_limit_bytes=VMEM_LIMIT),
        name="attn_out_stage",
    )(bounded, q, k, vt, g_attn, y_pool, x, mod3, w_out_bf)


def _pad_heads(w, width):
    kdim = w.shape[0]
    w = w.reshape(kdim, N_HEADS, width)
    w = jnp.pad(w, ((0, 0), (0, 0), (0, HEAD_PAD - width)))
    return w.reshape(kdim, N_HEADS * HEAD_PAD)


def _rope_partner(a):
    half = QK_ROPE // 2
    return jnp.concatenate([jnp.zeros_like(a[..., :QK_NOPE]), a[..., QK_NOPE + half:],
                            a[..., QK_NOPE:QK_NOPE + half]], axis=-1)


def kernel(x, c, positions, ada_w, ada_b, norm_g, w_in, pool_w, pool_scale, g_q_lat, w_uq,
           g_kv_lat, w_ukv, g_qnorm, g_knorm, w_out):
    b, s, d = x.shape
    bf = jnp.bfloat16
    cos_t, sin_t = _rope_tables(positions)
    mod3 = _ada_mod(c, ada_w[0], ada_b[0]).reshape(b, 3, d)

    wi = w_in[0]
    o_q, o_kr, o_ga = 2 * POOL_WIDTH, sum(IN_SPLITS[:4]), sum(IN_SPLITS[:5])
    w_lat = jnp.concatenate(
        [wi[:, o_q:o_kr], jnp.zeros((d, QK_NOPE), wi.dtype), wi[:, o_kr:o_ga],
         jnp.zeros((d, HEAD_PAD - QK_HEAD), wi.dtype)], axis=1).astype(bf)
    w_rest = jnp.concatenate([wi[:, :o_q], wi[:, o_ga:]], axis=1).astype(bf)
    wq3 = w_uq[0].reshape(Q_LORA, N_HEADS, QK_HEAD)
    w_uq_p = _pad_heads(w_uq[0], QK_HEAD).astype(bf)
    w_uqr_p = _pad_heads(_rope_partner(wq3).reshape(Q_LORA, -1), QK_HEAD).astype(bf)
    wkv = w_ukv[0].reshape(KV_LORA, N_HEADS, QK_NOPE + V_HEAD)
    w_uk_p = _pad_heads(wkv[..., :QK_NOPE].reshape(KV_LORA, -1), QK_NOPE).astype(bf)
    w_uvt = wkv[..., QK_NOPE:].reshape(KV_LORA, ATTN_WIDTH).T.astype(bf)
    pad_g = lambda g: jnp.pad(g, (0, HEAD_PAD - QK_HEAD)).reshape(1, HEAD_PAD)
    lane_head = jnp.arange(MXU_DIM) // HEAD_PAD
    ones_bd = (lane_head[:, None] == lane_head[None, :]).astype(bf)
    pw = pool_w[0].astype(bf)
    zero = jnp.zeros((POOL_GROUP, POOL_GROUP), bf)
    w_pool = jnp.stack([jnp.block([[pw[2 * j], zero], [zero, pw[2 * j + 1]]])
                        for j in range(POOL_WIDTH // MXU_DIM)])
    consts = (norm_g[0].reshape(1, d), w_lat, w_rest, g_q_lat[0].reshape(1, Q_LORA), w_uq_p,
              w_uqr_p, g_kv_lat[0].reshape(1, KV_LORA), w_uk_p, w_uvt, pad_g(g_qnorm[0]),
              pad_g(_rope_partner(g_qnorm[0])), pad_g(g_knorm[0]), ones_bd, w_pool,
              pool_scale[0].reshape(1, POOL_WIDTH))

    y_pool, q, k, vt, g_attn = _proj_stage(x, mod3, consts, cos_t, sin_t)
    score_bound = (math.log2(math.e) * QK_HEAD ** 0.5
                   * jnp.max(jnp.abs(g_qnorm[0])) * jnp.max(jnp.abs(g_knorm[0])))
    bounded = (score_bound <= SCORE_LIMIT).astype(jnp.int32).reshape(1)
    return _attn_stage(bounded, q, k, vt, g_attn, y_pool, x, mod3, w_out[0].astype(bf))
```

```python
import math

import jax
import jax.numpy as jnp
from jax import lax
from jax.experimental import pallas as pl
from jax.experimental.pallas import tpu as pltpu

D_MODEL = 1024
POOL_WINDOWS = (2, 4, 8, 16)
POOL_WIDTH = 512
POOL_GROUP = 128
N_HEADS = 8
QK_NOPE = 64
QK_ROPE = 32
QK_HEAD = 96
V_HEAD = 64
ATTN_WIDTH = 512
Q_LORA = 256
KV_LORA = 128
IN_SPLITS = (512, 512, 256, 128, 32, 512)
ROPE_THETA = 10000.0
EPS = 1e-6

LANES = 128
SUBLANES = 8
HEAD_PAD = LANES
MXU_DIM = 256
POOL_HALO = max(POOL_WINDOWS) // 2
VMEM_LIMIT = 56 * 1024 * 1024
ROPE_LO = QK_NOPE + QK_ROPE // 2
SCORE_LIMIT = 64.0
PROJ_TILE = 512
ATTN_TILE = 512
KEY_CHUNK = 256
SCORE_LOOKAHEAD = 2

assert POOL_HALO == SUBLANES


def _silu(x):
    return x * jax.nn.sigmoid(x)


def _rope_kernel(pos_ref, inv_ref, cos_ref, sin_ref):
    ang = pos_ref[0].astype(jnp.float32) * inv_ref[...]
    cos_ref[0] = jnp.cos(ang)
    sin_ref[0] = jnp.sin(ang)


def _rope_tables(positions):
    b, s = positions.shape
    half = QK_ROPE // 2
    inv = ROPE_THETA ** (-jnp.arange(0, QK_ROPE, 2, dtype=jnp.float32) / QK_ROPE)
    tab = pl.BlockSpec((1, half, s), lambda i: (i, 0, 0))
    return pl.pallas_call(
        _rope_kernel,
        grid=(b,),
        in_specs=[pl.BlockSpec((1, 1, s), lambda i: (i, 0, 0)),
                  pl.BlockSpec((half, 1), lambda i: (0, 0))],
        out_specs=[tab, tab],
        out_shape=[jax.ShapeDtypeStruct((b, half, s), jnp.float32)] * 2,
        name="rope_tables",
    )(positions.reshape(b, 1, s), inv.reshape(half, 1))


def _head_tables(cos_t, sin_t):
    tm = cos_t.shape[1]
    ones = lambda n: jnp.ones((n, tm), jnp.float32)
    zeros = lambda n: jnp.zeros((n, tm), jnp.float32)
    cosf = jnp.concatenate([ones(QK_NOPE), cos_t, cos_t, ones(HEAD_PAD - QK_HEAD)], axis=0)
    sinf = jnp.concatenate([zeros(QK_NOPE), -sin_t, sin_t, zeros(HEAD_PAD - QK_HEAD)], axis=0)
    return cosf.T, sinf.T


def _mod_kernel(c_ref, w_ref, b_ref, o_ref):
    c_act = _silu(c_ref[...])
    o_ref[...] = jnp.dot(c_act, w_ref[...], preferred_element_type=jnp.float32,
                         precision=lax.Precision.HIGHEST) + b_ref[...]


def _ada_mod(c, ada_w, ada_b):
    b, d = c.shape
    n = ada_w.shape[1]
    tn = 512
    return pl.pallas_call(
        _mod_kernel,
        grid=(n // tn,),
        in_specs=[pl.BlockSpec((b, d), lambda j: (0, 0)),
                  pl.BlockSpec((d, tn), lambda j: (0, j)),
                  pl.BlockSpec((1, tn), lambda j: (0, j))],
        out_specs=pl.BlockSpec((b, tn), lambda j: (0, j)),
        out_shape=jax.ShapeDtypeStruct((b, n), jnp.float32),
        name="ada_mod",
    )(c, ada_w, ada_b.reshape(1, n))


def _rms(x, n):
    return lax.rsqrt(jnp.sum(x * x, axis=-1, keepdims=True) * (1.0 / n) + EPS)


def _head_rms(xp, ones_ref):
    sq = (xp * xp).astype(jnp.bfloat16)
    parts = []
    for j in range(xp.shape[1] // MXU_DIM):
        sl = slice(j * MXU_DIM, (j + 1) * MXU_DIM)
        parts.append(jnp.dot(sq[:, sl], ones_ref[...], preferred_element_type=jnp.float32))
    ssq = jnp.concatenate(parts, axis=-1)
    return lax.rsqrt(ssq * (1.0 / QK_HEAD) + EPS)


def _window_sum(u, w):
    acc = u + pltpu.roll(u, 1, 0)
    span = 2
    n = u.shape[0]
    while span < w:
        half = span // 2
        acc = pltpu.roll(acc, n - half, 0) + pltpu.roll(acc, half, 0)
        span *= 2
    return acc


def _pool_branch(u, u_halo, g, t0, seq_len, wp_ref, ps_ref):
    tm = u.shape[0]
    t = t0 + lax.broadcasted_iota(jnp.int32, (tm, 1), 0)
    u_ext = jnp.concatenate([u_halo[:POOL_HALO], u, u_halo[POOL_HALO:]], axis=0)
    diffs = []
    for gi, w in enumerate(POOL_WINDOWS):
        sl = slice(gi * POOL_GROUP, (gi + 1) * POOL_GROUP)
        wsum = _window_sum(u_ext[:, sl], w)[POOL_HALO:POOL_HALO + tm]
        cnt = jnp.minimum(t + (w - w // 2), seq_len) - jnp.maximum(t - w // 2, 0)
        diffs.append((wsum / cnt.astype(jnp.float32) - u[:, sl]).astype(jnp.bfloat16))
    ys = []
    for j in range(POOL_WIDTH // MXU_DIM):
        d2 = jnp.concatenate(diffs[2 * j:2 * j + 2], axis=-1)
        ys.append(jnp.dot(d2, wp_ref[j], preferred_element_type=jnp.float32))
    y = jnp.concatenate(ys, axis=-1)
    return y * ps_ref[...] * _silu(g)


def _proj_kernel(x_ref, xp_ref, xn_ref, mod_ref, ng_ref, wlat_ref, wrest_ref, gq_ref, wuqt_ref,
                 gkv_ref, wuk_ref, wuvt_ref, gqn_ref, gkn_ref, ones_ref,
                 wp_ref, ps_ref, cos_ref, sin_ref,
                 yp_ref, qt_ref, k_ref, vt_ref, ga_ref):
    i = pl.program_id(1)
    tm = x_ref.shape[1]
    seq_len = tm * pl.num_programs(1)
    shift = mod_ref[0, 0:1, :]
    gain = ng_ref[...] * (1.0 + mod_ref[0, 1:2, :])

    def norm_mod(x):
        return ((x * _rms(x, D_MODEL)) * gain + shift).astype(jnp.bfloat16)

    xn = norm_mod(x_ref[0])
    lat = jnp.dot(xn, wlat_ref[...], preferred_element_type=jnp.float32)
    ug = jnp.dot(xn, wrest_ref[:, :2 * POOL_WIDTH], preferred_element_type=jnp.float32)
    xh = norm_mod(jnp.concatenate([xp_ref[0], xn_ref[0]], axis=0))
    u_halo = jnp.dot(xh, wrest_ref[:, :POOL_WIDTH], preferred_element_type=jnp.float32)

    q_lat = lat[:, :Q_LORA]
    kv_lat = lat[:, Q_LORA:Q_LORA + KV_LORA]
    kr_blk = lat[:, Q_LORA + KV_LORA:]
    qn_t = (q_lat * _rms(q_lat, Q_LORA) * gq_ref[...]).T.astype(jnp.bfloat16)
    kvn = (kv_lat * _rms(kv_lat, KV_LORA) * gkv_ref[...]).astype(jnp.bfloat16)
    q_raw_t = jnp.dot(wuqt_ref[...], qn_t, preferred_element_type=jnp.float32)
    k_nope = jnp.dot(kvn, wuk_ref[...], preferred_element_type=jnp.float32)
    vt = lax.dot_general(wuvt_ref[...], kvn, (((1,), (1,)), ((), ())),
                         preferred_element_type=jnp.float32)
    vt_ref[0] = vt.astype(jnp.bfloat16)

    row = lax.broadcasted_iota(jnp.int32, (2 * POOL_HALO, 1), 0)
    has_prev = (i > 0).astype(jnp.float32)
    has_next = (i < pl.num_programs(1) - 1).astype(jnp.float32)
    u_halo = u_halo * jnp.where(row < POOL_HALO, has_prev, has_next)
    yp = _pool_branch(ug[:, :POOL_WIDTH], u_halo, ug[:, POOL_WIDTH:], i * tm, seq_len,
                      wp_ref, ps_ref)
    yp_ref[0] = yp.astype(jnp.bfloat16)

    kc = k_nope + jnp.concatenate([kr_blk] * N_HEADS, axis=-1)
    rk = _head_rms(kc, ones_ref)
    ga_ref[0] = jnp.dot(xn, wrest_ref[:, 2 * POOL_WIDTH:], preferred_element_type=jnp.float32)

    cos_t = cos_ref[0]
    sin_t = sin_ref[0]
    qc = QK_HEAD ** -0.5 * math.log2(math.e)
    half = QK_ROPE // 2
    pad_rows = jnp.zeros((HEAD_PAD - QK_HEAD, tm), jnp.float32)
    for h in range(N_HEADS):
        blk = q_raw_t[h * HEAD_PAD:(h + 1) * HEAD_PAD]
        r = lax.rsqrt(jnp.sum(blk * blk, axis=0, keepdims=True) * (1.0 / QK_HEAD) + EPS) * qc
        gb = blk * gqn_ref[...]
        x1 = gb[QK_NOPE:QK_NOPE + half]
        x2 = gb[QK_NOPE + half:QK_HEAD]
        qh_t = jnp.concatenate([gb[:QK_NOPE], x1 * cos_t - x2 * sin_t, x2 * cos_t + x1 * sin_t,
                                pad_rows], axis=0) * r
        qt_ref[0, h * HEAD_PAD:(h + 1) * HEAD_PAD, :] = qh_t.astype(jnp.bfloat16)

    cosf, sinf = _head_tables(cos_t, sin_t)
    k_cos = cosf * gkn_ref[...]
    lane = lax.broadcasted_iota(jnp.int32, (1, HEAD_PAD), 1)
    kg = kr_blk * gkn_ref[...]
    k_rot = jnp.where(lane < ROPE_LO, pltpu.roll(kg, HEAD_PAD - QK_ROPE // 2, 1),
                      pltpu.roll(kg, QK_ROPE // 2, 1)) * sinf
    for h in range(N_HEADS):
        sl = slice(h * HEAD_PAD, (h + 1) * HEAD_PAD)
        kh = rk[:, sl] * (kc[:, sl] * k_cos + k_rot)
        k_ref[0, :, sl] = kh.astype(jnp.bfloat16)


def _proj_stage(x, mod3, consts, cos_t, sin_t):
    b, s, d = x.shape
    tm = PROJ_TILE
    hp = N_HEADS * HEAD_PAD
    halo_blocks = tm // POOL_HALO
    last_block = s // POOL_HALO - 1
    tok = lambda w: pl.BlockSpec((1, tm, w), lambda bi, i: (bi, i, 0))
    full = lambda a: pl.BlockSpec(a.shape, lambda bi, i: (0,) * a.ndim)
    tab = pl.BlockSpec((1, QK_ROPE // 2, tm), lambda bi, i: (bi, 0, i))
    prev = pl.BlockSpec((1, POOL_HALO, d),
                        lambda bi, i: (bi, jnp.maximum(i * halo_blocks - 1, 0), 0))
    nxt = pl.BlockSpec((1, POOL_HALO, d),
                       lambda bi, i: (bi, jnp.minimum((i + 1) * halo_blocks, last_block), 0))
    return pl.pallas_call(
        _proj_kernel,
        grid=(b, s // tm),
        in_specs=[tok(d), prev, nxt, pl.BlockSpec((1, 3, d), lambda bi, i: (bi, 0, 0))]
                 + [full(a) for a in consts] + [tab, tab],
        out_specs=[tok(POOL_WIDTH), pl.BlockSpec((1, hp, tm), lambda bi, i: (bi, 0, i)), tok(hp),
                   pl.BlockSpec((1, ATTN_WIDTH, tm), lambda bi, i: (bi, 0, i)), tok(ATTN_WIDTH)],
        out_shape=[jax.ShapeDtypeStruct((b, s, POOL_WIDTH), jnp.bfloat16),
                   jax.ShapeDtypeStruct((b, hp, s), jnp.bfloat16),
                   jax.ShapeDtypeStruct((b, s, hp), jnp.bfloat16),
                   jax.ShapeDtypeStruct((b, ATTN_WIDTH, s), jnp.bfloat16),
                   jax.ShapeDtypeStruct((b, s, ATTN_WIDTH), jnp.float32)],
        compiler_params=pltpu.CompilerParams(
            dimension_semantics=("arbitrary", "arbitrary"), vmem_limit_bytes=VMEM_LIMIT),
        name="proj_stage",
    )(x, x, x, mod3, *consts, cos_t, sin_t)


def _attn_body(shift_max, qt_ref, k_ref, vt_ref, ga_ref, yp_ref, x_ref, mod_ref, wo_ref, o_ref):
    kc = k_ref.shape[1] if shift_max else KEY_CHUNK
    steps = [(h, j) for h in range(N_HEADS) for j in range(k_ref.shape[1] // kc)]

    def scores_t(h, j):
        sl = slice(h * HEAD_PAD, (h + 1) * HEAD_PAD)
        kh = k_ref[0, j * kc:(j + 1) * kc, sl]
        qh = qt_ref[0, sl, :]
        return jnp.dot(kh, qh, preferred_element_type=jnp.float32)

    outs = []
    ahead = [scores_t(*steps[n]) for n in range(min(SCORE_LOOKAHEAD, len(steps)))]
    for n, (h, j) in enumerate(steps):
        st = ahead.pop(0)
        if n + SCORE_LOOKAHEAD < len(steps):
            ahead.append(scores_t(*steps[n + SCORE_LOOKAHEAD]))
        if shift_max:
            st = st - jnp.max(st, axis=0, keepdims=True)
        p = jnp.exp2(st)
        l_c = jnp.sum(p, axis=0, keepdims=True)
        o_c = jnp.dot(vt_ref[0, h * V_HEAD:(h + 1) * V_HEAD, j * kc:(j + 1) * kc],
                      p.astype(jnp.bfloat16), preferred_element_type=jnp.float32)
        l, ot = (l_c, o_c) if j == 0 else (l + l_c, ot + o_c)
        if (j + 1) * kc == k_ref.shape[1]:
            outs.append(ot / l)
    o = jnp.concatenate(outs, axis=0).T
    ya = (o * _silu(ga_ref[0])).astype(jnp.bfloat16)
    ycat = jnp.concatenate([yp_ref[0], ya], axis=-1)
    y = jnp.dot(ycat, wo_ref[...], preferred_element_type=jnp.float32)
    gate = mod_ref[0, 2:3, :]
    o_ref[0] = x_ref[0] + gate * y


def _attn_kernel(bounded_ref, *refs):
    @pl.when(bounded_ref[0] != 0)
    def _():
        _attn_body(False, *refs)

    @pl.when(bounded_ref[0] == 0)
    def _():
        _attn_body(True, *refs)


def _attn_stage(bounded, q, k, vt, g_attn, y_pool, x, mod3, w_out_bf):
    b, s, d = x.shape
    tq = ATTN_TILE
    hp = N_HEADS * HEAD_PAD
    tok = lambda w: pl.BlockSpec((1, tq, w), lambda bi, i: (bi, i, 0))
    return pl.pallas_call(
        _attn_kernel,
        grid=(b, s // tq),
        in_specs=[pl.BlockSpec(memory_space=pltpu.SMEM),
                  pl.BlockSpec((1, hp, tq), lambda bi, i: (bi, 0, i)),
                  pl.BlockSpec((1, s, hp), lambda bi, i: (bi, 0, 0)),
                  pl.BlockSpec((1, ATTN_WIDTH, s), lambda bi, i: (bi, 0, 0)),
                  tok(ATTN_WIDTH), tok(POOL_WIDTH), tok(d),
                  pl.BlockSpec((1, 3, d), lambda bi, i: (bi, 0, 0)),
                  pl.BlockSpec(w_out_bf.shape, lambda bi, i: (0, 0))],
        out_specs=tok(d),
        out_shape=jax.ShapeDtypeStruct((b, s, d), jnp.float32),
        compiler_params=pltpu.CompilerParams(
            dimension_semantics=("arbitrary", "arbitrary"), vmem_limit_bytes=VMEM_LIMIT),
        name="attn_out_stage",
    )(bounded, q, k, vt, g_attn, y_pool, x, mod3, w_out_bf)


def _pad_heads(w, width):
    kdim = w.shape[0]
    w = w.reshape(kdim, N_HEADS, width)
    w = jnp.pad(w, ((0, 0), (0, 0), (0, HEAD_PAD - width)))
    return w.reshape(kdim, N_HEADS * HEAD_PAD)


def kernel(x, c, positions, ada_w, ada_b, norm_g, w_in, pool_w, pool_scale, g_q_lat, w_uq,
           g_kv_lat, w_ukv, g_qnorm, g_knorm, w_out):
    b, s, d = x.shape
    bf = jnp.bfloat16
    cos_t, sin_t = _rope_tables(positions)
    mod3 = _ada_mod(c, ada_w[0], ada_b[0]).reshape(b, 3, d)

    wi = w_in[0]
    o_q, o_kr, o_ga = 2 * POOL_WIDTH, sum(IN_SPLITS[:4]), sum(IN_SPLITS[:5])
    w_lat = jnp.concatenate(
        [wi[:, o_q:o_kr], jnp.zeros((d, QK_NOPE), wi.dtype), wi[:, o_kr:o_ga],
         jnp.zeros((d, HEAD_PAD - QK_HEAD), wi.dtype)], axis=1).astype(bf)
    w_rest = jnp.concatenate([wi[:, :o_q], wi[:, o_ga:]], axis=1).astype(bf)
    w_uqt_p = _pad_heads(w_uq[0], QK_HEAD).T.astype(bf)
    wkv = w_ukv[0].reshape(KV_LORA, N_HEADS, QK_NOPE + V_HEAD)
    w_uk_p = _pad_heads(wkv[..., :QK_NOPE].reshape(KV_LORA, -1), QK_NOPE).astype(bf)
    w_uvt = wkv[..., QK_NOPE:].reshape(KV_LORA, ATTN_WIDTH).T.astype(bf)
    pad_g = lambda g: jnp.pad(g, (0, HEAD_PAD - QK_HEAD)).reshape(1, HEAD_PAD)
    lane_head = jnp.arange(MXU_DIM) // HEAD_PAD
    ones_bd = (lane_head[:, None] == lane_head[None, :]).astype(bf)
    pw = pool_w[0].astype(bf)
    zero = jnp.zeros((POOL_GROUP, POOL_GROUP), bf)
    w_pool = jnp.stack([jnp.block([[pw[2 * j], zero], [zero, pw[2 * j + 1]]])
                        for j in range(POOL_WIDTH // MXU_DIM)])
    g_qn_col = jnp.broadcast_to(pad_g(g_qnorm[0]).reshape(HEAD_PAD, 1), (HEAD_PAD, PROJ_TILE))
    consts = (norm_g[0].reshape(1, d), w_lat, w_rest, g_q_lat[0].reshape(1, Q_LORA), w_uqt_p,
              g_kv_lat[0].reshape(1, KV_LORA), w_uk_p, w_uvt, g_qn_col, pad_g(g_knorm[0]),
              ones_bd, w_pool, pool_scale[0].reshape(1, POOL_WIDTH))

    y_pool, qt, k, vt, g_attn = _proj_stage(x, mod3, consts, cos_t, sin_t)
    score_bound = (math.log2(math.e) * QK_HEAD ** 0.5
                   * jnp.max(jnp.abs(g_qnorm[0])) * jnp.max(jnp.abs(g_knorm[0])))
    bounded = (score_bound <= SCORE_LIMIT).astype(jnp.int32).reshape(1)
    return _attn_stage(bounded, qt, k, vt, g_attn, y_pool, x, mod3, w_out[0].astype(bf))
```

```python
import math

import jax
import jax.numpy as jnp
from jax import lax
from jax.experimental import pallas as pl
from jax.experimental.pallas import tpu as pltpu

D_MODEL = 1024
POOL_WINDOWS = (2, 4, 8, 16)
POOL_WIDTH = 512
POOL_GROUP = 128
N_HEADS = 8
QK_NOPE = 64
QK_ROPE = 32
QK_HEAD = 96
V_HEAD = 64
ATTN_WIDTH = 512
Q_LORA = 256
KV_LORA = 128
IN_SPLITS = (512, 512, 256, 128, 32, 512)
ROPE_THETA = 10000.0
EPS = 1e-6

LANES = 128
SUBLANES = 8
HEAD_PAD = LANES
MXU_DIM = 256
POOL_HALO = max(POOL_WINDOWS) // 2
VMEM_LIMIT = 56 * 1024 * 1024
LAT_WIDTH = 512
SCORE_LIMIT = 64.0
PROJ_TILE = 512
ATTN_TILE = 512
KEY_CHUNK = 256
SCORE_LOOKAHEAD = 2

assert POOL_HALO == SUBLANES


def _silu(x):
    return x * jax.nn.sigmoid(x)


def _rope_kernel(pos_ref, inv_ref, cos_ref, sin_ref):
    ang = pos_ref[0].astype(jnp.float32) * inv_ref[...]
    cos_ref[0] = jnp.cos(ang)
    sin_ref[0] = jnp.sin(ang)


def _rope_tables(positions):
    b, s = positions.shape
    half = QK_ROPE // 2
    inv = ROPE_THETA ** (-jnp.arange(0, QK_ROPE, 2, dtype=jnp.float32) / QK_ROPE)
    tab = pl.BlockSpec((1, half, s), lambda i: (i, 0, 0))
    return pl.pallas_call(
        _rope_kernel,
        grid=(b,),
        in_specs=[pl.BlockSpec((1, 1, s), lambda i: (i, 0, 0)),
                  pl.BlockSpec((half, 1), lambda i: (0, 0))],
        out_specs=[tab, tab],
        out_shape=[jax.ShapeDtypeStruct((b, half, s), jnp.float32)] * 2,
        name="rope_tables",
    )(positions.reshape(b, 1, s), inv.reshape(half, 1))


def _mod_kernel(c_ref, w_ref, b_ref, o_ref):
    c_act = _silu(c_ref[...])
    o_ref[...] = jnp.dot(c_act, w_ref[...], preferred_element_type=jnp.float32,
                         precision=lax.Precision.HIGHEST) + b_ref[...]


def _ada_mod(c, ada_w, ada_b):
    b, d = c.shape
    n = ada_w.shape[1]
    tn = 512
    return pl.pallas_call(
        _mod_kernel,
        grid=(n // tn,),
        in_specs=[pl.BlockSpec((b, d), lambda j: (0, 0)),
                  pl.BlockSpec((d, tn), lambda j: (0, j)),
                  pl.BlockSpec((1, tn), lambda j: (0, j))],
        out_specs=pl.BlockSpec((b, tn), lambda j: (0, j)),
        out_shape=jax.ShapeDtypeStruct((b, n), jnp.float32),
        name="ada_mod",
    )(c, ada_w, ada_b.reshape(1, n))


def _rms(x, n):
    return lax.rsqrt(jnp.sum(x * x, axis=-1, keepdims=True) * (1.0 / n) + EPS)


def _window_sum(u, w):
    acc = u + pltpu.roll(u, 1, 0)
    span = 2
    n = u.shape[0]
    while span < w:
        half = span // 2
        acc = pltpu.roll(acc, n - half, 0) + pltpu.roll(acc, half, 0)
        span *= 2
    return acc


def _pool_branch(u, u_halo, g, t0, seq_len, wp_ref, ps_ref):
    tm = u.shape[0]
    u_ext = jnp.concatenate([u_halo[:POOL_HALO], u, u_halo[POOL_HALO:]], axis=0)
    edge = lax.broadcasted_iota(jnp.int32, (POOL_HALO, 1), 0)
    t_lo = t0 + edge
    t_hi = t0 + (tm - POOL_HALO) + edge
    diffs = []
    for gi, w in enumerate(POOL_WINDOWS):
        sl = slice(gi * POOL_GROUP, (gi + 1) * POOL_GROUP)
        mean = _window_sum(u_ext[:, sl], w)[POOL_HALO:POOL_HALO + tm] * (1.0 / w)
        fixes = []
        for t in (t_lo, t_hi):
            cnt = jnp.minimum(t + (w - w // 2), seq_len) - jnp.maximum(t - w // 2, 0)
            fixes.append(w / cnt.astype(jnp.float32))
        mean = jnp.concatenate([mean[:POOL_HALO] * fixes[0], mean[POOL_HALO:tm - POOL_HALO],
                                mean[tm - POOL_HALO:] * fixes[1]], axis=0)
        diffs.append((mean - u[:, sl]).astype(jnp.bfloat16))
    ys = []
    for j in range(POOL_WIDTH // MXU_DIM):
        d2 = jnp.concatenate(diffs[2 * j:2 * j + 2], axis=-1)
        ys.append(jnp.dot(d2, wp_ref[j], preferred_element_type=jnp.float32))
    y = jnp.concatenate(ys, axis=-1)
    return y * ps_ref[...] * _silu(g)


def _col_rms(x_t, n):
    return lax.rsqrt(jnp.sum(x_t * x_t, axis=0, keepdims=True) * (1.0 / n) + EPS)


def _rotary_t(x_t, cos_t, sin_t):
    half = QK_ROPE // 2
    x1, x2 = x_t[:half], x_t[half:]
    return jnp.concatenate([x1 * cos_t - x2 * sin_t, x2 * cos_t + x1 * sin_t], axis=0)


def _proj_kernel(x_ref, xp_ref, xn_ref, mod_ref, ng_ref, wlat_ref, wrest_ref, gq_ref, wuqt_ref,
                 gkv_ref, wukvt_ref, gqn_ref, gkn_ref, wp_ref, ps_ref, cos_ref, sin_ref,
                 yp_ref, qt_ref, k_ref, vt_ref, ga_ref):
    i = pl.program_id(1)
    tm = x_ref.shape[1]
    seq_len = tm * pl.num_programs(1)
    shift = mod_ref[0, 0:1, :]
    gain = ng_ref[...] * (1.0 + mod_ref[0, 1:2, :])

    def norm_mod(x):
        return ((x * _rms(x, D_MODEL)) * gain + shift).astype(jnp.bfloat16)

    xn = norm_mod(x_ref[0])
    lat = jnp.dot(xn, wlat_ref[...], preferred_element_type=jnp.float32)
    ug = jnp.dot(xn, wrest_ref[:, :2 * POOL_WIDTH], preferred_element_type=jnp.float32)
    xh = norm_mod(jnp.concatenate([xp_ref[0], xn_ref[0]], axis=0))
    u_halo = jnp.dot(xh, wrest_ref[:, :POOL_WIDTH], preferred_element_type=jnp.float32)
    ga_ref[0] = jnp.dot(xn, wrest_ref[:, 2 * POOL_WIDTH:], preferred_element_type=jnp.float32)

    lanes = lambda ref: jnp.concatenate([ref[...]] * (tm // LANES), axis=1)
    lat_t = lat.T
    q_lat_t = lat_t[:Q_LORA]
    kv_lat_t = lat_t[Q_LORA:Q_LORA + KV_LORA]
    kr_t = lat_t[Q_LORA + KV_LORA:Q_LORA + KV_LORA + QK_ROPE]
    qn_t = (q_lat_t * _col_rms(q_lat_t, Q_LORA) * lanes(gq_ref)).astype(jnp.bfloat16)
    kvn_t = (kv_lat_t * _col_rms(kv_lat_t, KV_LORA) * lanes(gkv_ref)).astype(jnp.bfloat16)
    q_raw_t = jnp.dot(wuqt_ref[...], qn_t, preferred_element_type=jnp.float32)
    kv_t = jnp.dot(wukvt_ref[...], kvn_t, preferred_element_type=jnp.float32)
    vt_ref[0] = kv_t[ATTN_WIDTH:].astype(jnp.bfloat16)

    row = lax.broadcasted_iota(jnp.int32, (2 * POOL_HALO, 1), 0)
    has_prev = (i > 0).astype(jnp.float32)
    has_next = (i < pl.num_programs(1) - 1).astype(jnp.float32)
    u_halo = u_halo * jnp.where(row < POOL_HALO, has_prev, has_next)
    yp = _pool_branch(ug[:, :POOL_WIDTH], u_halo, ug[:, POOL_WIDTH:], i * tm, seq_len,
                      wp_ref, ps_ref)
    yp_ref[0] = yp.astype(jnp.bfloat16)

    cos_t = cos_ref[0]
    sin_t = sin_ref[0]
    pad_rows = jnp.zeros((HEAD_PAD - QK_HEAD, tm), jnp.float32)
    qc = QK_HEAD ** -0.5 * math.log2(math.e)
    gqn = lanes(gqn_ref)
    for h in range(N_HEADS):
        blk = q_raw_t[h * HEAD_PAD:h * HEAD_PAD + QK_HEAD]
        gb = blk * gqn
        qh_t = jnp.concatenate([gb[:QK_NOPE], _rotary_t(gb[QK_NOPE:], cos_t, sin_t), pad_rows],
                               axis=0) * (_col_rms(blk, QK_HEAD) * qc)
        qt_ref[0, h * HEAD_PAD:(h + 1) * HEAD_PAD, :] = qh_t.astype(jnp.bfloat16)

    gkn = lanes(gkn_ref)
    k_rope_t = _rotary_t(kr_t * gkn[QK_NOPE:], cos_t, sin_t)
    kr_ssq = jnp.sum(kr_t * kr_t, axis=0, keepdims=True)
    for h in range(N_HEADS):
        nope = kv_t[h * QK_NOPE:(h + 1) * QK_NOPE]
        ssq = jnp.sum(nope * nope, axis=0, keepdims=True) + kr_ssq
        r = lax.rsqrt(ssq * (1.0 / QK_HEAD) + EPS)
        kh_t = jnp.concatenate([nope * gkn[:QK_NOPE], k_rope_t, pad_rows], axis=0) * r
        k_ref[0, :, h * HEAD_PAD:(h + 1) * HEAD_PAD] = kh_t.T.astype(jnp.bfloat16)


def _proj_stage(x, mod3, consts, cos_t, sin_t):
    b, s, d = x.shape
    tm = PROJ_TILE
    hp = N_HEADS * HEAD_PAD
    halo_blocks = tm // POOL_HALO
    last_block = s // POOL_HALO - 1
    tok = lambda w: pl.BlockSpec((1, tm, w), lambda bi, i: (bi, i, 0))
    full = lambda a: pl.BlockSpec(a.shape, lambda bi, i: (0,) * a.ndim)
    tab = pl.BlockSpec((1, QK_ROPE // 2, tm), lambda bi, i: (bi, 0, i))
    prev = pl.BlockSpec((1, POOL_HALO, d),
                        lambda bi, i: (bi, jnp.maximum(i * halo_blocks - 1, 0), 0))
    nxt = pl.BlockSpec((1, POOL_HALO, d),
                       lambda bi, i: (bi, jnp.minimum((i + 1) * halo_blocks, last_block), 0))
    return pl.pallas_call(
        _proj_kernel,
        grid=(b, s // tm),
        in_specs=[tok(d), prev, nxt, pl.BlockSpec((1, 3, d), lambda bi, i: (bi, 0, 0))]
                 + [full(a) for a in consts] + [tab, tab],
        out_specs=[tok(POOL_WIDTH), pl.BlockSpec((1, hp, tm), lambda bi, i: (bi, 0, i)), tok(hp),
                   pl.BlockSpec((1, ATTN_WIDTH, tm), lambda bi, i: (bi, 0, i)), tok(ATTN_WIDTH)],
        out_shape=[jax.ShapeDtypeStruct((b, s, POOL_WIDTH), jnp.bfloat16),
                   jax.ShapeDtypeStruct((b, hp, s), jnp.bfloat16),
                   jax.ShapeDtypeStruct((b, s, hp), jnp.bfloat16),
                   jax.ShapeDtypeStruct((b, ATTN_WIDTH, s), jnp.bfloat16),
                   jax.ShapeDtypeStruct((b, s, ATTN_WIDTH), jnp.float32)],
        compiler_params=pltpu.CompilerParams(
            dimension_semantics=("arbitrary", "arbitrary"), vmem_limit_bytes=VMEM_LIMIT),
        name="proj_stage",
    )(x, x, x, mod3, *consts, cos_t, sin_t)


def _attn_body(shift_max, qt_ref, k_ref, vt_ref, ga_ref, yp_ref, x_ref, mod_ref, wo_ref, o_ref):
    kc = k_ref.shape[1] if shift_max else KEY_CHUNK
    steps = [(h, j) for h in range(N_HEADS) for j in range(k_ref.shape[1] // kc)]

    def scores_t(h, j):
        sl = slice(h * HEAD_PAD, (h + 1) * HEAD_PAD)
        kh = k_ref[0, j * kc:(j + 1) * kc, sl]
        qh = qt_ref[0, sl, :]
        return jnp.dot(kh, qh, preferred_element_type=jnp.float32)

    outs = []
    ahead = [scores_t(*steps[n]) for n in range(min(SCORE_LOOKAHEAD, len(steps)))]
    for n, (h, j) in enumerate(steps):
        st = ahead.pop(0)
        if n + SCORE_LOOKAHEAD < len(steps):
            ahead.append(scores_t(*steps[n + SCORE_LOOKAHEAD]))
        if shift_max:
            st = st - jnp.max(st, axis=0, keepdims=True)
        p = jnp.exp2(st)
        l_c = jnp.sum(p, axis=0, keepdims=True)
        o_c = jnp.dot(vt_ref[0, h * V_HEAD:(h + 1) * V_HEAD, j * kc:(j + 1) * kc],
                      p.astype(jnp.bfloat16), preferred_element_type=jnp.float32)
        l, ot = (l_c, o_c) if j == 0 else (l + l_c, ot + o_c)
        if (j + 1) * kc == k_ref.shape[1]:
            outs.append(ot / l)
    o = jnp.concatenate(outs, axis=0).T
    ya = (o * _silu(ga_ref[0])).astype(jnp.bfloat16)
    ycat = jnp.concatenate([yp_ref[0], ya], axis=-1)
    y = jnp.dot(ycat, wo_ref[...], preferred_element_type=jnp.float32)
    gate = mod_ref[0, 2:3, :]
    o_ref[0] = x_ref[0] + gate * y


def _attn_kernel(bounded_ref, *refs):
    @pl.when(bounded_ref[0] != 0)
    def _():
        _attn_body(False, *refs)

    @pl.when(bounded_ref[0] == 0)
    def _():
        _attn_body(True, *refs)


def _attn_stage(bounded, q, k, vt, g_attn, y_pool, x, mod3, w_out_bf):
    b, s, d = x.shape
    tq = ATTN_TILE
    hp = N_HEADS * HEAD_PAD
    tok = lambda w: pl.BlockSpec((1, tq, w), lambda bi, i: (bi, i, 0))
    return pl.pallas_call(
        _attn_kernel,
        grid=(b, s // tq),
        in_specs=[pl.BlockSpec(memory_space=pltpu.SMEM),
                  pl.BlockSpec((1, hp, tq), lambda bi, i: (bi, 0, i)),
                  pl.BlockSpec((1, s, hp), lambda bi, i: (bi, 0, 0)),
                  pl.BlockSpec((1, ATTN_WIDTH, s), lambda bi, i: (bi, 0, 0)),
                  tok(ATTN_WIDTH), tok(POOL_WIDTH), tok(d),
                  pl.BlockSpec((1, 3, d), lambda bi, i: (bi, 0, 0)),
                  pl.BlockSpec(w_out_bf.shape, lambda bi, i: (0, 0))],
        out_specs=tok(d),
        out_shape=jax.ShapeDtypeStruct((b, s, d), jnp.float32),
        compiler_params=pltpu.CompilerParams(
            dimension_semantics=("arbitrary", "arbitrary"), vmem_limit_bytes=VMEM_LIMIT),
        name="attn_out_stage",
    )(bounded, q, k, vt, g_attn, y_pool, x, mod3, w_out_bf)


def _pad_heads(w, width):
    kdim = w.shape[0]
    w = w.reshape(kdim, N_HEADS, width)
    w = jnp.pad(w, ((0, 0), (0, 0), (0, HEAD_PAD - width)))
    return w.reshape(kdim, N_HEADS * HEAD_PAD)


def kernel(x, c, positions, ada_w, ada_b, norm_g, w_in, pool_w, pool_scale, g_q_lat, w_uq,
           g_kv_lat, w_ukv, g_qnorm, g_knorm, w_out):
    b, s, d = x.shape
    bf = jnp.bfloat16
    cos_t, sin_t = _rope_tables(positions)
    mod3 = _ada_mod(c, ada_w[0], ada_b[0]).reshape(b, 3, d)

    wi = w_in[0]
    o_q, o_kr, o_ga = 2 * POOL_WIDTH, sum(IN_SPLITS[:4]), sum(IN_SPLITS[:5])
    w_lat = jnp.concatenate(
        [wi[:, o_q:o_ga], jnp.zeros((d, LAT_WIDTH - (o_ga - o_q)), wi.dtype)], axis=1).astype(bf)
    w_rest = jnp.concatenate([wi[:, :o_q], wi[:, o_ga:]], axis=1).astype(bf)
    w_uqt_p = _pad_heads(w_uq[0], QK_HEAD).T.astype(bf)
    wkv = w_ukv[0].reshape(KV_LORA, N_HEADS, QK_NOPE + V_HEAD)
    w_ukvt = jnp.concatenate([wkv[..., :QK_NOPE].reshape(KV_LORA, -1),
                              wkv[..., QK_NOPE:].reshape(KV_LORA, -1)], axis=1).T.astype(bf)
    col = lambda g: jnp.broadcast_to(g.reshape(-1, 1), (g.shape[0], LANES))
    pw = pool_w[0].astype(bf)
    zero = jnp.zeros((POOL_GROUP, POOL_GROUP), bf)
    w_pool = jnp.stack([jnp.block([[pw[2 * j], zero], [zero, pw[2 * j + 1]]])
                        for j in range(POOL_WIDTH // MXU_DIM)])
    consts = (norm_g[0].reshape(1, d), w_lat, w_rest, col(g_q_lat[0]), w_uqt_p, col(g_kv_lat[0]),
              w_ukvt, col(g_qnorm[0]), col(g_knorm[0]), w_pool,
              pool_scale[0].reshape(1, POOL_WIDTH))

    y_pool, qt, k, vt, g_attn = _proj_stage(x, mod3, consts, cos_t, sin_t)
    score_bound = (math.log2(math.e) * QK_HEAD ** 0.5
                   * jnp.max(jnp.abs(g_qnorm[0])) * jnp.max(jnp.abs(g_knorm[0])))
    bounded = (score_bound <= SCORE_LIMIT).astype(jnp.int32).reshape(1)
    return _attn_stage(bounded, qt, k, vt, g_attn, y_pool, x, mod3, w_out[0].astype(bf))
```

```python
import math

import jax
import jax.numpy as jnp
from jax import lax
from jax.experimental import pallas as pl
from jax.experimental.pallas import tpu as pltpu

D_MODEL = 1024
POOL_WINDOWS = (2, 4, 8, 16)
POOL_WIDTH = 512
POOL_GROUP = 128
N_HEADS = 8
QK_NOPE = 64
QK_ROPE = 32
QK_HEAD = 96
V_HEAD = 64
ATTN_WIDTH = 512
Q_LORA = 256
KV_LORA = 128
IN_SPLITS = (512, 512, 256, 128, 32, 512)
ROPE_THETA = 10000.0
EPS = 1e-6

LANES = 128
SUBLANES = 8
HEAD_PAD = LANES
MXU_DIM = 256
POOL_HALO = max(POOL_WINDOWS) // 2
VMEM_LIMIT = 56 * 1024 * 1024
LAT_WIDTH = 512
SCORE_LIMIT = 64.0
PROJ_TILE = 1024
PROJ_SUBTILES = (256, 256, 256, 256)
ATTN_TILE = 1024
ATTN_SUBTILE = 512
ATTN_SUBTILE_SHIFTED = 256
KEY_CHUNK = 256
SCORE_LOOKAHEAD = 2

assert POOL_HALO == SUBLANES


def _silu(x):
    return x * jax.nn.sigmoid(x)


def _rope_kernel(pos_ref, inv_ref, cos_ref, sin_ref):
    ang = pos_ref[0].astype(jnp.float32) * inv_ref[...]
    cos_ref[0] = jnp.cos(ang)
    sin_ref[0] = jnp.sin(ang)


def _rope_tables(positions):
    b, s = positions.shape
    half = QK_ROPE // 2
    inv = ROPE_THETA ** (-jnp.arange(0, QK_ROPE, 2, dtype=jnp.float32) / QK_ROPE)
    tab = pl.BlockSpec((1, half, s), lambda i: (i, 0, 0))
    return pl.pallas_call(
        _rope_kernel,
        grid=(b,),
        in_specs=[pl.BlockSpec((1, 1, s), lambda i: (i, 0, 0)),
                  pl.BlockSpec((half, 1), lambda i: (0, 0))],
        out_specs=[tab, tab],
        out_shape=[jax.ShapeDtypeStruct((b, half, s), jnp.float32)] * 2,
        name="rope_tables",
    )(positions.reshape(b, 1, s), inv.reshape(half, 1))


def _mod_kernel(c_ref, w_ref, b_ref, o_ref):
    c_act = _silu(c_ref[...])
    o_ref[...] = jnp.dot(c_act, w_ref[...], preferred_element_type=jnp.float32,
                         precision=lax.Precision.HIGHEST) + b_ref[...]


def _ada_mod(c, ada_w, ada_b):
    b, d = c.shape
    n = ada_w.shape[1]
    tn = 512
    return pl.pallas_call(
        _mod_kernel,
        grid=(n // tn,),
        in_specs=[pl.BlockSpec((b, d), lambda j: (0, 0)),
                  pl.BlockSpec((d, tn), lambda j: (0, j)),
                  pl.BlockSpec((1, tn), lambda j: (0, j))],
        out_specs=pl.BlockSpec((b, tn), lambda j: (0, j)),
        out_shape=jax.ShapeDtypeStruct((b, n), jnp.float32),
        name="ada_mod",
    )(c, ada_w, ada_b.reshape(1, n))


def _rms(x, n):
    return lax.rsqrt(jnp.sum(x * x, axis=-1, keepdims=True) * (1.0 / n) + EPS)


def _window_sum(u, w):
    acc = u + pltpu.roll(u, 1, 0)
    span = 2
    n = u.shape[0]
    while span < w:
        half = span // 2
        acc = pltpu.roll(acc, n - half, 0) + pltpu.roll(acc, half, 0)
        span *= 2
    return acc


def _pool_branch(u, u_halo, g, t0, seq_len, wp_ref, ps_ref):
    tm = u.shape[0]
    u_ext = jnp.concatenate([u_halo[:POOL_HALO], u, u_halo[POOL_HALO:]], axis=0)
    edge = lax.broadcasted_iota(jnp.int32, (POOL_HALO, 1), 0)
    t_lo = t0 + edge
    t_hi = t0 + (tm - POOL_HALO) + edge
    diffs = []
    for gi, w in enumerate(POOL_WINDOWS):
        sl = slice(gi * POOL_GROUP, (gi + 1) * POOL_GROUP)
        mean = _window_sum(u_ext[:, sl], w)[POOL_HALO:POOL_HALO + tm] * (1.0 / w)
        fixes = []
        for t in (t_lo, t_hi):
            cnt = jnp.minimum(t + (w - w // 2), seq_len) - jnp.maximum(t - w // 2, 0)
            fixes.append(w / cnt.astype(jnp.float32))
        mean = jnp.concatenate([mean[:POOL_HALO] * fixes[0], mean[POOL_HALO:tm - POOL_HALO],
                                mean[tm - POOL_HALO:] * fixes[1]], axis=0)
        diffs.append((mean - u[:, sl]).astype(jnp.bfloat16))
    ys = []
    for j in range(POOL_WIDTH // MXU_DIM):
        d2 = jnp.concatenate(diffs[2 * j:2 * j + 2], axis=-1)
        ys.append(jnp.dot(d2, wp_ref[j], preferred_element_type=jnp.float32))
    y = jnp.concatenate(ys, axis=-1)
    return y * ps_ref[...] * _silu(g)


def _col_rms(x_t, n):
    return lax.rsqrt(jnp.sum(x_t * x_t, axis=0, keepdims=True) * (1.0 / n) + EPS)


def _rotary_t(x_t, cos_t, sin_t):
    half = QK_ROPE // 2
    x1, x2 = x_t[:half], x_t[half:]
    return jnp.concatenate([x1 * cos_t - x2 * sin_t, x2 * cos_t + x1 * sin_t], axis=0)


def _proj_kernel(x_ref, xp_ref, xn_ref, mod_ref, ng_ref, wlat_ref, wrest_ref, gq_ref, wuqt_ref,
                 gkv_ref, wukvt_ref, gqn_ref, gkn_ref, wp_ref, ps_ref, cos_ref, sin_ref,
                 yp_ref, qt_ref, k_ref, vt_ref, ga_ref):
    i = pl.program_id(1)
    tm = x_ref.shape[1]
    seq_len = tm * pl.num_programs(1)
    shift = mod_ref[0, 0:1, :]
    gain = ng_ref[...] * (1.0 + mod_ref[0, 1:2, :])
    lanes = lambda ref, ts: jnp.concatenate([ref[...]] * (ts // LANES), axis=1)

    def norm_mod(x):
        return ((x * _rms(x, D_MODEL)) * gain + shift).astype(jnp.bfloat16)

    def matmul_phase(r0, ts):
        rows = slice(r0, r0 + ts)
        xn = norm_mod(x_ref[0, rows, :])
        lat = jnp.dot(xn, wlat_ref[...], preferred_element_type=jnp.float32)
        ug = jnp.dot(xn, wrest_ref[:, :2 * POOL_WIDTH], preferred_element_type=jnp.float32)
        ga_ref[0, rows, :] = jnp.dot(xn, wrest_ref[:, 2 * POOL_WIDTH:],
                                     preferred_element_type=jnp.float32)
        lat_t = lat.T
        q_lat_t = lat_t[:Q_LORA]
        kv_lat_t = lat_t[Q_LORA:Q_LORA + KV_LORA]
        kr_t = lat_t[Q_LORA + KV_LORA:Q_LORA + KV_LORA + QK_ROPE]
        qn_t = (q_lat_t * _col_rms(q_lat_t, Q_LORA) * lanes(gq_ref, ts)).astype(jnp.bfloat16)
        kvn_t = (kv_lat_t * _col_rms(kv_lat_t, KV_LORA)
                 * lanes(gkv_ref, ts)).astype(jnp.bfloat16)
        q_raw_t = jnp.dot(wuqt_ref[...], qn_t, preferred_element_type=jnp.float32)
        kv_t = jnp.dot(wukvt_ref[...], kvn_t, preferred_element_type=jnp.float32)
        vt_ref[0, :, rows] = kv_t[ATTN_WIDTH:].astype(jnp.bfloat16)
        return ug, q_raw_t, kv_t[:ATTN_WIDTH], kr_t

    def vector_phase(r0, ug, u_before, u_after, q_raw_t, k_nope_t, kr_t):
        ts = ug.shape[0]
        rows = slice(r0, r0 + ts)
        yp = _pool_branch(ug[:, :POOL_WIDTH], jnp.concatenate([u_before, u_after], axis=0),
                          ug[:, POOL_WIDTH:], i * tm + r0, seq_len, wp_ref, ps_ref)
        yp_ref[0, rows, :] = yp.astype(jnp.bfloat16)

        cos_t = cos_ref[0, :, rows]
        sin_t = sin_ref[0, :, rows]
        pad_rows = jnp.zeros((HEAD_PAD - QK_HEAD, ts), jnp.float32)
        qc = QK_HEAD ** -0.5 * math.log2(math.e)
        gqn = lanes(gqn_ref, ts)
        for h in range(N_HEADS):
            blk = q_raw_t[h * HEAD_PAD:h * HEAD_PAD + QK_HEAD]
            gb = blk * gqn
            qh_t = jnp.concatenate(
                [gb[:QK_NOPE], _rotary_t(gb[QK_NOPE:], cos_t, sin_t), pad_rows],
                axis=0) * (_col_rms(blk, QK_HEAD) * qc)
            qt_ref[0, h * HEAD_PAD:(h + 1) * HEAD_PAD, rows] = qh_t.astype(jnp.bfloat16)

        gkn = lanes(gkn_ref, ts)
        k_rope_t = _rotary_t(kr_t * gkn[QK_NOPE:], cos_t, sin_t)
        kr_ssq = jnp.sum(kr_t * kr_t, axis=0, keepdims=True)
        for h in range(N_HEADS):
            nope = k_nope_t[h * QK_NOPE:(h + 1) * QK_NOPE]
            ssq = jnp.sum(nope * nope, axis=0, keepdims=True) + kr_ssq
            r = lax.rsqrt(ssq * (1.0 / QK_HEAD) + EPS)
            kh_t = jnp.concatenate([nope * gkn[:QK_NOPE], k_rope_t, pad_rows], axis=0) * r
            k_ref[0, rows, h * HEAD_PAD:(h + 1) * HEAD_PAD] = kh_t.T.astype(jnp.bfloat16)

    xh = norm_mod(jnp.concatenate([xp_ref[0], xn_ref[0]], axis=0))
    u_halo = jnp.dot(xh, wrest_ref[:, :POOL_WIDTH], preferred_element_type=jnp.float32)
    row = lax.broadcasted_iota(jnp.int32, (2 * POOL_HALO, 1), 0)
    has_prev = (i > 0).astype(jnp.float32)
    has_next = (i < pl.num_programs(1) - 1).astype(jnp.float32)
    u_halo = u_halo * jnp.where(row < POOL_HALO, has_prev, has_next)

    starts = [sum(PROJ_SUBTILES[:n]) for n in range(len(PROJ_SUBTILES))]
    n_sub = len(PROJ_SUBTILES)
    phases = [matmul_phase(starts[0], PROJ_SUBTILES[0])]
    for n in range(n_sub):
        if n + 1 < n_sub:
            phases.append(matmul_phase(starts[n + 1], PROJ_SUBTILES[n + 1]))
        ug, q_raw_t, k_nope_t, kr_t = phases[n]
        u_before = (u_halo[:POOL_HALO] if n == 0
                    else phases[n - 1][0][PROJ_SUBTILES[n - 1] - POOL_HALO:, :POOL_WIDTH])
        u_after = (u_halo[POOL_HALO:] if n + 1 == n_sub
                   else phases[n + 1][0][:POOL_HALO, :POOL_WIDTH])
        vector_phase(starts[n], ug, u_before, u_after, q_raw_t, k_nope_t, kr_t)


def _proj_stage(x, mod3, consts, cos_t, sin_t):
    b, s, d = x.shape
    tm = PROJ_TILE
    hp = N_HEADS * HEAD_PAD
    halo_blocks = tm // POOL_HALO
    last_block = s // POOL_HALO - 1
    tok = lambda w: pl.BlockSpec((1, tm, w), lambda bi, i: (bi, i, 0))
    full = lambda a: pl.BlockSpec(a.shape, lambda bi, i: (0,) * a.ndim)
    tab = pl.BlockSpec((1, QK_ROPE // 2, tm), lambda bi, i: (bi, 0, i))
    prev = pl.BlockSpec((1, POOL_HALO, d),
                        lambda bi, i: (bi, jnp.maximum(i * halo_blocks - 1, 0), 0))
    nxt = pl.BlockSpec((1, POOL_HALO, d),
                       lambda bi, i: (bi, jnp.minimum((i + 1) * halo_blocks, last_block), 0))
    return pl.pallas_call(
        _proj_kernel,
        grid=(b, s // tm),
        in_specs=[tok(d), prev, nxt, pl.BlockSpec((1, 3, d), lambda bi, i: (bi, 0, 0))]
                 + [full(a) for a in consts] + [tab, tab],
        out_specs=[tok(POOL_WIDTH), pl.BlockSpec((1, hp, tm), lambda bi, i: (bi, 0, i)), tok(hp),
                   pl.BlockSpec((1, ATTN_WIDTH, tm), lambda bi, i: (bi, 0, i)), tok(ATTN_WIDTH)],
        out_shape=[jax.ShapeDtypeStruct((b, s, POOL_WIDTH), jnp.bfloat16),
                   jax.ShapeDtypeStruct((b, hp, s), jnp.bfloat16),
                   jax.ShapeDtypeStruct((b, s, hp), jnp.bfloat16),
                   jax.ShapeDtypeStruct((b, ATTN_WIDTH, s), jnp.bfloat16),
                   jax.ShapeDtypeStruct((b, s, ATTN_WIDTH), jnp.float32)],
        compiler_params=pltpu.CompilerParams(
            dimension_semantics=("arbitrary", "arbitrary"), vmem_limit_bytes=VMEM_LIMIT),
        name="proj_stage",
    )(x, x, x, mod3, *consts, cos_t, sin_t)


def _attn_body(shift_max, qt_ref, k_ref, vt_ref, ga_ref, yp_ref, x_ref, mod_ref, wo_ref, o_ref):
    kc = k_ref.shape[1] if shift_max else KEY_CHUNK
    tq = ATTN_SUBTILE_SHIFTED if shift_max else ATTN_SUBTILE
    n_chunks = k_ref.shape[1] // kc
    steps = [(q0, h, j) for q0 in range(0, qt_ref.shape[2], tq) for h in range(N_HEADS)
             for j in range(n_chunks)]

    def scores_t(q0, h, j):
        sl = slice(h * HEAD_PAD, (h + 1) * HEAD_PAD)
        kh = k_ref[0, j * kc:(j + 1) * kc, sl]
        qh = qt_ref[0, sl, q0:q0 + tq]
        return jnp.dot(kh, qh, preferred_element_type=jnp.float32)

    def finish(q0, outs):
        rows = slice(q0, q0 + tq)
        o = jnp.concatenate(outs, axis=0).T
        ya = (o * _silu(ga_ref[0, rows, :])).astype(jnp.bfloat16)
        ycat = jnp.concatenate([yp_ref[0, rows, :], ya], axis=-1)
        y = jnp.dot(ycat, wo_ref[...], preferred_element_type=jnp.float32)
        o_ref[0, rows, :] = x_ref[0, rows, :] + mod_ref[0, 2:3, :] * y

    outs = []
    ahead = [scores_t(*steps[n]) for n in range(min(SCORE_LOOKAHEAD, len(steps)))]
    for n, (q0, h, j) in enumerate(steps):
        st = ahead.pop(0)
        if n + SCORE_LOOKAHEAD < len(steps):
            ahead.append(scores_t(*steps[n + SCORE_LOOKAHEAD]))
        if shift_max:
            st = st - jnp.max(st, axis=0, keepdims=True)
        p = jnp.exp2(st)
        l_c = jnp.sum(p, axis=0, keepdims=True)
        o_c = jnp.dot(vt_ref[0, h * V_HEAD:(h + 1) * V_HEAD, j * kc:(j + 1) * kc],
                      p.astype(jnp.bfloat16), preferred_element_type=jnp.float32)
        l, ot = (l_c, o_c) if j == 0 else (l + l_c, ot + o_c)
        if j + 1 == n_chunks:
            outs.append(ot / l)
            if h + 1 == N_HEADS:
                finish(q0, outs)
                outs = []


def _attn_kernel(bounded_ref, *refs):
    @pl.when(bounded_ref[0] != 0)
    def _():
        _attn_body(False, *refs)

    @pl.when(bounded_ref[0] == 0)
    def _():
        _attn_body(True, *refs)


def _attn_stage(bounded, q, k, vt, g_attn, y_pool, x, mod3, w_out_bf):
    b, s, d = x.shape
    tq = ATTN_TILE
    hp = N_HEADS * HEAD_PAD
    tok = lambda w: pl.BlockSpec((1, tq, w), lambda bi, i: (bi, i, 0))
    return pl.pallas_call(
        _attn_kernel,
        grid=(b, s // tq),
        in_specs=[pl.BlockSpec(memory_space=pltpu.SMEM),
                  pl.BlockSpec((1, hp, tq), lambda bi, i: (bi, 0, i)),
                  pl.BlockSpec((1, s, hp), lambda bi, i: (bi, 0, 0)),
                  pl.BlockSpec((1, ATTN_WIDTH, s), lambda bi, i: (bi, 0, 0)),
                  tok(ATTN_WIDTH), tok(POOL_WIDTH), tok(d),
                  pl.BlockSpec((1, 3, d), lambda bi, i: (bi, 0, 0)),
                  pl.BlockSpec(w_out_bf.shape, lambda bi, i: (0, 0))],
        out_specs=tok(d),
        out_shape=jax.ShapeDtypeStruct((b, s, d), jnp.float32),
        compiler_params=pltpu.CompilerParams(
            dimension_semantics=("arbitrary", "arbitrary"), vmem_limit_bytes=VMEM_LIMIT),
        name="attn_out_stage",
    )(bounded, q, k, vt, g_attn, y_pool, x, mod3, w_out_bf)


def _pad_heads(w, width):
    kdim = w.shape[0]
    w = w.reshape(kdim, N_HEADS, width)
    w = jnp.pad(w, ((0, 0), (0, 0), (0, HEAD_PAD - width)))
    return w.reshape(kdim, N_HEADS * HEAD_PAD)


def kernel(x, c, positions, ada_w, ada_b, norm_g, w_in, pool_w, pool_scale, g_q_lat, w_uq,
           g_kv_lat, w_ukv, g_qnorm, g_knorm, w_out):
    b, s, d = x.shape
    bf = jnp.bfloat16
    cos_t, sin_t = _rope_tables(positions)
    mod3 = _ada_mod(c, ada_w[0], ada_b[0]).reshape(b, 3, d)

    wi = w_in[0]
    o_q, o_kr, o_ga = 2 * POOL_WIDTH, sum(IN_SPLITS[:4]), sum(IN_SPLITS[:5])
    w_lat = jnp.concatenate(
        [wi[:, o_q:o_ga], jnp.zeros((d, LAT_WIDTH - (o_ga - o_q)), wi.dtype)], axis=1).astype(bf)
    w_rest = jnp.concatenate([wi[:, :o_q], wi[:, o_ga:]], axis=1).astype(bf)
    w_uqt_p = _pad_heads(w_uq[0], QK_HEAD).T.astype(bf)
    wkv = w_ukv[0].reshape(KV_LORA, N_HEADS, QK_NOPE + V_HEAD)
    w_ukvt = jnp.concatenate([wkv[..., :QK_NOPE].reshape(KV_LORA, -1),
                              wkv[..., QK_NOPE:].reshape(KV_LORA, -1)], axis=1).T.astype(bf)
    col = lambda g: jnp.broadcast_to(g.reshape(-1, 1), (g.shape[0], LANES))
    pw = pool_w[0].astype(bf)
    zero = jnp.zeros((POOL_GROUP, POOL_GROUP), bf)
    w_pool = jnp.stack([jnp.block([[pw[2 * j], zero], [zero, pw[2 * j + 1]]])
                        for j in range(POOL_WIDTH // MXU_DIM)])
    consts = (norm_g[0].reshape(1, d), w_lat, w_rest, col(g_q_lat[0]), w_uqt_p, col(g_kv_lat[0]),
              w_ukvt, col(g_qnorm[0]), col(g_knorm[0]), w_pool,
              pool_scale[0].reshape(1, POOL_WIDTH))

    y_pool, qt, k, vt, g_attn = _proj_stage(x, mod3, consts, cos_t, sin_t)
    score_bound = (math.log2(math.e) * QK_HEAD ** 0.5
                   * jnp.max(jnp.abs(g_qnorm[0])) * jnp.max(jnp.abs(g_knorm[0])))
    bounded = (score_bound <= SCORE_LIMIT).astype(jnp.int32).reshape(1)
    return _attn_stage(bounded, qt, k, vt, g_attn, y_pool, x, mod3, w_out[0].astype(bf))
```

```python
import math

import jax
import jax.numpy as jnp
from jax import lax
from jax.experimental import pallas as pl
from jax.experimental.pallas import tpu as pltpu

D_MODEL = 1024
POOL_WINDOWS = (2, 4, 8, 16)
POOL_WIDTH = 512
POOL_GROUP = 128
N_HEADS = 8
QK_NOPE = 64
QK_ROPE = 32
QK_HEAD = 96
V_HEAD = 64
ATTN_WIDTH = 512
Q_LORA = 256
KV_LORA = 128
IN_SPLITS = (512, 512, 256, 128, 32, 512)
ROPE_THETA = 10000.0
EPS = 1e-6

LANES = 128
SUBLANES = 8
HEAD_PAD = LANES
MXU_DIM = 256
POOL_HALO = max(POOL_WINDOWS) // 2
VMEM_LIMIT = 56 * 1024 * 1024
LAT_WIDTH = 512
SCORE_LIMIT = 64.0
PROJ_TILE = 1024
PROJ_SUBTILES = (256, 256, 256, 256)
ATTN_TILE = 512
ATTN_SUBTILE = 512
ATTN_SUBTILE_SHIFTED = 256
KEY_CHUNK = 256
SCORE_LOOKAHEAD = 2

assert POOL_HALO == SUBLANES


def _silu(x):
    return x * jax.nn.sigmoid(x)


def _mod_kernel(c_ref, w_ref, b_ref, o_ref):
    c_act = _silu(c_ref[...])
    o_ref[...] = jnp.dot(c_act, w_ref[...], preferred_element_type=jnp.float32,
                         precision=lax.Precision.HIGHEST) + b_ref[...]


def _ada_mod(c, ada_w, ada_b):
    b, d = c.shape
    n = ada_w.shape[1]
    tn = 512
    return pl.pallas_call(
        _mod_kernel,
        grid=(n // tn,),
        in_specs=[pl.BlockSpec((b, d), lambda j: (0, 0)),
                  pl.BlockSpec((d, tn), lambda j: (0, j)),
                  pl.BlockSpec((1, tn), lambda j: (0, j))],
        out_specs=pl.BlockSpec((b, tn), lambda j: (0, j)),
        out_shape=jax.ShapeDtypeStruct((b, n), jnp.float32),
        name="ada_mod",
    )(c, ada_w, ada_b.reshape(1, n))


def _rms(x, n):
    return lax.rsqrt(jnp.sum(x * x, axis=-1, keepdims=True) * (1.0 / n) + EPS)


def _window_sum(u_ext, w):
    n = u_ext.shape[0]
    tm = n - 2 * POOL_HALO
    ahead = lambda a, k: pltpu.roll(a, n - k, 0)
    behind = lambda a, k: pltpu.roll(a, k, 0)
    inner = lambda a: a[POOL_HALO:POOL_HALO + tm]
    if w == 2:
        return inner(u_ext + behind(u_ext, 1))
    fwd = u_ext + ahead(u_ext, 1)
    span = 2
    while 2 * span < w:
        fwd = fwd + ahead(fwd, span)
        span *= 2
    if span == POOL_HALO:
        return fwd[:tm] + inner(fwd)
    return inner(fwd + behind(fwd, span))


def _pool_branch(u, u_halo, g, t0, seq_len, wp_ref, ps_ref):
    tm = u.shape[0]
    u_ext = jnp.concatenate([u_halo[:POOL_HALO], u, u_halo[POOL_HALO:]], axis=0)
    edge = lax.broadcasted_iota(jnp.int32, (POOL_HALO, 1), 0)
    t_lo = t0 + edge
    t_hi = t0 + (tm - POOL_HALO) + edge
    diffs = []
    for gi, w in enumerate(POOL_WINDOWS):
        sl = slice(gi * POOL_GROUP, (gi + 1) * POOL_GROUP)
        mean = _window_sum(u_ext[:, sl], w) * (1.0 / w)
        fixes = []
        for t in (t_lo, t_hi):
            cnt = jnp.minimum(t + (w - w // 2), seq_len) - jnp.maximum(t - w // 2, 0)
            fixes.append(w / cnt.astype(jnp.float32))
        mean = jnp.concatenate([mean[:POOL_HALO] * fixes[0], mean[POOL_HALO:tm - POOL_HALO],
                                mean[tm - POOL_HALO:] * fixes[1]], axis=0)
        diffs.append((mean - u[:, sl]).astype(jnp.bfloat16))
    ys = []
    for j in range(POOL_WIDTH // MXU_DIM):
        d2 = jnp.concatenate(diffs[2 * j:2 * j + 2], axis=-1)
        ys.append(jnp.dot(d2, wp_ref[j], preferred_element_type=jnp.float32))
    y = jnp.concatenate(ys, axis=-1)
    return y * ps_ref[...] * _silu(g)


def _col_rms(x_t, n):
    return lax.rsqrt(jnp.sum(x_t * x_t, axis=0, keepdims=True) * (1.0 / n) + EPS)


def _rotary_t(x_t, cos_t, sin_t):
    half = QK_ROPE // 2
    x1, x2 = x_t[:half], x_t[half:]
    return jnp.concatenate([x1 * cos_t - x2 * sin_t, x2 * cos_t + x1 * sin_t], axis=0)


def _proj_kernel(x_ref, xp_ref, xn_ref, mod_ref, ng_ref, wlat_ref, wrest_ref, gq_ref, wuqt_ref,
                 gkv_ref, wukvt_ref, gqn_ref, gkn_ref, wp_ref, ps_ref, inv_ref, pos_ref,
                 yp_ref, qt_ref, k_ref, vt_ref, ga_ref):
    i = pl.program_id(1)
    tm = x_ref.shape[1]
    seq_len = tm * pl.num_programs(1)
    shift = mod_ref[0, 0:1, :]
    gain = ng_ref[...] * (1.0 + mod_ref[0, 1:2, :])
    lanes = lambda ref, ts: jnp.concatenate([ref[...]] * (ts // LANES), axis=1)

    def norm_mod(x):
        return ((x * _rms(x, D_MODEL)) * gain + shift).astype(jnp.bfloat16)

    def matmul_phase(r0, ts):
        rows = slice(r0, r0 + ts)
        xn = norm_mod(x_ref[0, rows, :])
        lat = jnp.dot(xn, wlat_ref[...], preferred_element_type=jnp.float32)
        ug = jnp.dot(xn, wrest_ref[:, :2 * POOL_WIDTH], preferred_element_type=jnp.float32)
        ga_ref[0, rows, :] = jnp.dot(xn, wrest_ref[:, 2 * POOL_WIDTH:],
                                     preferred_element_type=jnp.float32)
        lat_t = lat.T
        q_lat_t = lat_t[:Q_LORA]
        kv_lat_t = lat_t[Q_LORA:Q_LORA + KV_LORA]
        kr_t = lat_t[Q_LORA + KV_LORA:Q_LORA + KV_LORA + QK_ROPE]
        qn_t = (q_lat_t * _col_rms(q_lat_t, Q_LORA) * lanes(gq_ref, ts)).astype(jnp.bfloat16)
        kvn_t = (kv_lat_t * _col_rms(kv_lat_t, KV_LORA)
                 * lanes(gkv_ref, ts)).astype(jnp.bfloat16)
        q_raw_t = jnp.dot(wuqt_ref[...], qn_t, preferred_element_type=jnp.float32)
        kv_t = jnp.dot(wukvt_ref[...], kvn_t, preferred_element_type=jnp.float32)
        vt_ref[0, :, rows] = kv_t[ATTN_WIDTH:].astype(jnp.bfloat16)
        return ug, q_raw_t, kv_t[:ATTN_WIDTH], kr_t

    def vector_phase(r0, ug, u_before, u_after, q_raw_t, k_nope_t, kr_t):
        ts = ug.shape[0]
        rows = slice(r0, r0 + ts)
        yp = _pool_branch(ug[:, :POOL_WIDTH], jnp.concatenate([u_before, u_after], axis=0),
                          ug[:, POOL_WIDTH:], i * tm + r0, seq_len, wp_ref, ps_ref)
        yp_ref[0, rows, :] = yp.astype(jnp.bfloat16)

        ang = pos_ref[0, :, rows].astype(jnp.float32) * inv_ref[...]
        cos_t = jnp.cos(ang)
        sin_t = jnp.sin(ang)
        pad_rows = jnp.zeros((HEAD_PAD - QK_HEAD, ts), jnp.float32)
        qc = QK_HEAD ** -0.5 * math.log2(math.e)
        gqn = lanes(gqn_ref, ts)
        for h in range(N_HEADS):
            blk = q_raw_t[h * HEAD_PAD:h * HEAD_PAD + QK_HEAD]
            gb = blk * gqn
            qh_t = jnp.concatenate(
                [gb[:QK_NOPE], _rotary_t(gb[QK_NOPE:], cos_t, sin_t), pad_rows],
                axis=0) * (_col_rms(blk, QK_HEAD) * qc)
            qt_ref[0, h * HEAD_PAD:(h + 1) * HEAD_PAD, rows] = qh_t.astype(jnp.bfloat16)

        gkn = lanes(gkn_ref, ts)
        k_rope_t = _rotary_t(kr_t * gkn[QK_NOPE:], cos_t, sin_t)
        kr_ssq = jnp.sum(kr_t * kr_t, axis=0, keepdims=True)
        for h in range(N_HEADS):
            nope = k_nope_t[h * QK_NOPE:(h + 1) * QK_NOPE]
            ssq = jnp.sum(nope * nope, axis=0, keepdims=True) + kr_ssq
            r = lax.rsqrt(ssq * (1.0 / QK_HEAD) + EPS)
            kh_t = jnp.concatenate([nope * gkn[:QK_NOPE], k_rope_t, pad_rows], axis=0) * r
            k_ref[0, rows, h * HEAD_PAD:(h + 1) * HEAD_PAD] = kh_t.T.astype(jnp.bfloat16)

    xh = norm_mod(jnp.concatenate([xp_ref[0], xn_ref[0]], axis=0))
    u_halo = jnp.dot(xh, wrest_ref[:, :POOL_WIDTH], preferred_element_type=jnp.float32)
    row = lax.broadcasted_iota(jnp.int32, (2 * POOL_HALO, 1), 0)
    has_prev = (i > 0).astype(jnp.float32)
    has_next = (i < pl.num_programs(1) - 1).astype(jnp.float32)
    u_halo = u_halo * jnp.where(row < POOL_HALO, has_prev, has_next)

    starts = [sum(PROJ_SUBTILES[:n]) for n in range(len(PROJ_SUBTILES))]
    n_sub = len(PROJ_SUBTILES)
    phases = [matmul_phase(starts[0], PROJ_SUBTILES[0])]
    for n in range(n_sub):
        if n + 1 < n_sub:
            phases.append(matmul_phase(starts[n + 1], PROJ_SUBTILES[n + 1]))
        ug, q_raw_t, k_nope_t, kr_t = phases[n]
        u_before = (u_halo[:POOL_HALO] if n == 0
                    else phases[n - 1][0][PROJ_SUBTILES[n - 1] - POOL_HALO:, :POOL_WIDTH])
        u_after = (u_halo[POOL_HALO:] if n + 1 == n_sub
                   else phases[n + 1][0][:POOL_HALO, :POOL_WIDTH])
        vector_phase(starts[n], ug, u_before, u_after, q_raw_t, k_nope_t, kr_t)


def _proj_stage(x, mod3, consts, positions):
    b, s, d = x.shape
    tm = PROJ_TILE
    hp = N_HEADS * HEAD_PAD
    halo_blocks = tm // POOL_HALO
    last_block = s // POOL_HALO - 1
    tok = lambda w: pl.BlockSpec((1, tm, w), lambda bi, i: (bi, i, 0))
    full = lambda a: pl.BlockSpec(a.shape, lambda bi, i: (0,) * a.ndim)
    pos = pl.BlockSpec((1, 1, tm), lambda bi, i: (bi, 0, i))
    prev = pl.BlockSpec((1, POOL_HALO, d),
                        lambda bi, i: (bi, jnp.maximum(i * halo_blocks - 1, 0), 0))
    nxt = pl.BlockSpec((1, POOL_HALO, d),
                       lambda bi, i: (bi, jnp.minimum((i + 1) * halo_blocks, last_block), 0))
    return pl.pallas_call(
        _proj_kernel,
        grid=(b, s // tm),
        in_specs=[tok(d), prev, nxt, pl.BlockSpec((1, 3, d), lambda bi, i: (bi, 0, 0))]
                 + [full(a) for a in consts] + [pos],
        out_specs=[tok(POOL_WIDTH), pl.BlockSpec((1, hp, tm), lambda bi, i: (bi, 0, i)), tok(hp),
                   pl.BlockSpec((1, ATTN_WIDTH, tm), lambda bi, i: (bi, 0, i)), tok(ATTN_WIDTH)],
        out_shape=[jax.ShapeDtypeStruct((b, s, POOL_WIDTH), jnp.bfloat16),
                   jax.ShapeDtypeStruct((b, hp, s), jnp.bfloat16),
                   jax.ShapeDtypeStruct((b, s, hp), jnp.bfloat16),
                   jax.ShapeDtypeStruct((b, ATTN_WIDTH, s), jnp.bfloat16),
                   jax.ShapeDtypeStruct((b, s, ATTN_WIDTH), jnp.float32)],
        compiler_params=pltpu.CompilerParams(
            dimension_semantics=("arbitrary", "arbitrary"), vmem_limit_bytes=VMEM_LIMIT),
        name="proj_stage",
    )(x, x, x, mod3, *consts, positions.reshape(b, 1, s))


def _attn_body(shift_max, qt_ref, k_ref, vt_ref, ga_ref, yp_ref, x_ref, mod_ref, wo_ref, o_ref):
    kc = k_ref.shape[1] if shift_max else KEY_CHUNK
    tq = ATTN_SUBTILE_SHIFTED if shift_max else ATTN_SUBTILE
    n_chunks = k_ref.shape[1] // kc
    steps = [(q0, h, j) for q0 in range(0, qt_ref.shape[2], tq) for h in range(N_HEADS)
             for j in range(n_chunks)]

    def scores_t(q0, h, j):
        sl = slice(h * HEAD_PAD, (h + 1) * HEAD_PAD)
        kh = k_ref[0, j * kc:(j + 1) * kc, sl]
        qh = qt_ref[0, sl, q0:q0 + tq]
        return jnp.dot(kh, qh, preferred_element_type=jnp.float32)

    def finish(q0, outs):
        rows = slice(q0, q0 + tq)
        o = jnp.concatenate(outs, axis=0).T
        ya = (o * _silu(ga_ref[0, rows, :])).astype(jnp.bfloat16)
        ycat = jnp.concatenate([yp_ref[0, rows, :], ya], axis=-1)
        y = jnp.dot(ycat, wo_ref[...], preferred_element_type=jnp.float32)
        o_ref[0, rows, :] = x_ref[0, rows, :] + mod_ref[0, 2:3, :] * y

    outs = []
    ahead = [scores_t(*steps[n]) for n in range(min(SCORE_LOOKAHEAD, len(steps)))]
    for n, (q0, h, j) in enumerate(steps):
        st = ahead.pop(0)
        if n + SCORE_LOOKAHEAD < len(steps):
            ahead.append(scores_t(*steps[n + SCORE_LOOKAHEAD]))
        if shift_max:
            st = st - jnp.max(st, axis=0, keepdims=True)
        p = jnp.exp2(st)
        l_c = jnp.sum(p, axis=0, keepdims=True)
        o_c = jnp.dot(vt_ref[0, h * V_HEAD:(h + 1) * V_HEAD, j * kc:(j + 1) * kc],
                      p.astype(jnp.bfloat16), preferred_element_type=jnp.float32)
        l, ot = (l_c, o_c) if j == 0 else (l + l_c, ot + o_c)
        if j + 1 == n_chunks:
            outs.append(ot / l)
            if h + 1 == N_HEADS:
                finish(q0, outs)
                outs = []


def _attn_kernel(bounded_ref, *refs):
    @pl.when(bounded_ref[0] != 0)
    def _():
        _attn_body(False, *refs)

    @pl.when(bounded_ref[0] == 0)
    def _():
        _attn_body(True, *refs)


def _attn_stage(bounded, q, k, vt, g_attn, y_pool, x, mod3, w_out_bf):
    b, s, d = x.shape
    tq = ATTN_TILE
    hp = N_HEADS * HEAD_PAD
    tok = lambda w: pl.BlockSpec((1, tq, w), lambda bi, i: (bi, i, 0))
    return pl.pallas_call(
        _attn_kernel,
        grid=(b, s // tq),
        in_specs=[pl.BlockSpec(memory_space=pltpu.SMEM),
                  pl.BlockSpec((1, hp, tq), lambda bi, i: (bi, 0, i)),
                  pl.BlockSpec((1, s, hp), lambda bi, i: (bi, 0, 0)),
                  pl.BlockSpec((1, ATTN_WIDTH, s), lambda bi, i: (bi, 0, 0)),
                  tok(ATTN_WIDTH), tok(POOL_WIDTH), tok(d),
                  pl.BlockSpec((1, 3, d), lambda bi, i: (bi, 0, 0)),
                  pl.BlockSpec(w_out_bf.shape, lambda bi, i: (0, 0))],
        out_specs=tok(d),
        out_shape=jax.ShapeDtypeStruct((b, s, d), jnp.float32),
        compiler_params=pltpu.CompilerParams(
            dimension_semantics=("arbitrary", "arbitrary"), vmem_limit_bytes=VMEM_LIMIT),
        name="attn_out_stage",
    )(bounded, q, k, vt, g_attn, y_pool, x, mod3, w_out_bf)


def _pad_heads(w, width):
    kdim = w.shape[0]
    w = w.reshape(kdim, N_HEADS, width)
    w = jnp.pad(w, ((0, 0), (0, 0), (0, HEAD_PAD - width)))
    return w.reshape(kdim, N_HEADS * HEAD_PAD)


def kernel(x, c, positions, ada_w, ada_b, norm_g, w_in, pool_w, pool_scale, g_q_lat, w_uq,
           g_kv_lat, w_ukv, g_qnorm, g_knorm, w_out):
    b, s, d = x.shape
    bf = jnp.bfloat16
    mod3 = _ada_mod(c, ada_w[0], ada_b[0]).reshape(b, 3, d)

    wi = w_in[0]
    o_q, o_kr, o_ga = 2 * POOL_WIDTH, sum(IN_SPLITS[:4]), sum(IN_SPLITS[:5])
    w_lat = jnp.concatenate(
        [wi[:, o_q:o_ga], jnp.zeros((d, LAT_WIDTH - (o_ga - o_q)), wi.dtype)], axis=1).astype(bf)
    w_rest = jnp.concatenate([wi[:, :o_q], wi[:, o_ga:]], axis=1).astype(bf)
    w_uqt_p = _pad_heads(w_uq[0], QK_HEAD).T.astype(bf)
    wkv = w_ukv[0].reshape(KV_LORA, N_HEADS, QK_NOPE + V_HEAD)
    w_ukvt = jnp.concatenate([wkv[..., :QK_NOPE].reshape(KV_LORA, -1),
                              wkv[..., QK_NOPE:].reshape(KV_LORA, -1)], axis=1).T.astype(bf)
    col = lambda g: jnp.broadcast_to(g.reshape(-1, 1), (g.shape[0], LANES))
    pw = pool_w[0].astype(bf)
    zero = jnp.zeros((POOL_GROUP, POOL_GROUP), bf)
    w_pool = jnp.stack([jnp.block([[pw[2 * j], zero], [zero, pw[2 * j + 1]]])
                        for j in range(POOL_WIDTH // MXU_DIM)])
    inv_freq = ROPE_THETA ** (-jnp.arange(0, QK_ROPE, 2, dtype=jnp.float32) / QK_ROPE)
    consts = (norm_g[0].reshape(1, d), w_lat, w_rest, col(g_q_lat[0]), w_uqt_p, col(g_kv_lat[0]),
              w_ukvt, col(g_qnorm[0]), col(g_knorm[0]), w_pool,
              pool_scale[0].reshape(1, POOL_WIDTH), inv_freq.reshape(QK_ROPE // 2, 1))

    y_pool, qt, k, vt, g_attn = _proj_stage(x, mod3, consts, positions)
    score_bound = (math.log2(math.e) * QK_HEAD ** 0.5
                   * jnp.max(jnp.abs(g_qnorm[0])) * jnp.max(jnp.abs(g_knorm[0])))
    bounded = (score_bound <= SCORE_LIMIT).astype(jnp.int32).reshape(1)
    return _attn_stage(bounded, qt, k, vt, g_attn, y_pool, x, mod3, w_out[0].astype(bf))
```

```python
import math

import jax
import jax.numpy as jnp
from jax import lax
from jax.experimental import pallas as pl
from jax.experimental.pallas import tpu as pltpu

D_MODEL = 1024
POOL_WINDOWS = (2, 4, 8, 16)
POOL_WIDTH = 512
POOL_GROUP = 128
N_HEADS = 8
QK_NOPE = 64
QK_ROPE = 32
QK_HEAD = 96
V_HEAD = 64
ATTN_WIDTH = 512
Q_LORA = 256
KV_LORA = 128
IN_SPLITS = (512, 512, 256, 128, 32, 512)
ROPE_THETA = 10000.0
EPS = 1e-6

LANES = 128
SUBLANES = 8
HEAD_PAD = LANES
MXU_DIM = 256
POOL_HALO = max(POOL_WINDOWS) // 2
VMEM_LIMIT = 56 * 1024 * 1024
LAT_WIDTH = 512
SCORE_LIMIT = 64.0
PROJ_TILE = 1024
PROJ_SUBTILES = (256, 256, 256, 256)
ATTN_TILE = 512
ATTN_SUBTILE = 512
ATTN_SUBTILE_SHIFTED = 256
KEY_CHUNK = 256
SCORE_LOOKAHEAD = 2

G_QLAT = slice(0, Q_LORA)
G_KVLAT = slice(Q_LORA, Q_LORA + KV_LORA)
G_QNORM = slice(Q_LORA + KV_LORA, Q_LORA + KV_LORA + QK_HEAD)
G_KNORM = slice(Q_LORA + KV_LORA + QK_HEAD, Q_LORA + KV_LORA + 2 * QK_HEAD)

assert POOL_HALO == SUBLANES


def _silu(x):
    return x * jax.nn.sigmoid(x)


def _mod_kernel(c_ref, w_ref, b_ref, o_ref):
    c_act = _silu(c_ref[...])
    o_ref[...] = jnp.dot(c_act, w_ref[...], preferred_element_type=jnp.float32,
                         precision=lax.Precision.HIGHEST) + b_ref[...]


def _ada_mod(c, ada_w, ada_b):
    b, d = c.shape
    n = ada_w.shape[1]
    tn = 512
    return pl.pallas_call(
        _mod_kernel,
        grid=(n // tn,),
        in_specs=[pl.BlockSpec((b, d), lambda j: (0, 0)),
                  pl.BlockSpec((d, tn), lambda j: (0, j)),
                  pl.BlockSpec((1, tn), lambda j: (0, j))],
        out_specs=pl.BlockSpec((b, tn), lambda j: (0, j)),
        out_shape=jax.ShapeDtypeStruct((b, n), jnp.float32),
        name="ada_mod",
    )(c, ada_w, ada_b.reshape(1, n))


def _mod_row(mod_ref, part):
    return mod_ref[pl.ds(pl.program_id(0), 1), part * D_MODEL:(part + 1) * D_MODEL]


def _rms(x, n):
    return lax.rsqrt(jnp.sum(x * x, axis=-1, keepdims=True) * (1.0 / n) + EPS)


def _window_sum(u_ext, w):
    n = u_ext.shape[0]
    tm = n - 2 * POOL_HALO
    ahead = lambda a, k: pltpu.roll(a, n - k, 0)
    behind = lambda a, k: pltpu.roll(a, k, 0)
    inner = lambda a: a[POOL_HALO:POOL_HALO + tm]
    if w == 2:
        return inner(u_ext + behind(u_ext, 1))
    fwd = u_ext + ahead(u_ext, 1)
    span = 2
    while 2 * span < w:
        fwd = fwd + ahead(fwd, span)
        span *= 2
    if span == POOL_HALO:
        return fwd[:tm] + inner(fwd)
    return inner(fwd + behind(fwd, span))


def _pool_branch(u, u_halo, g, t0, seq_len, wp_ref, ps_ref):
    tm = u.shape[0]
    u_ext = jnp.concatenate([u_halo[:POOL_HALO], u, u_halo[POOL_HALO:]], axis=0)
    edge = lax.broadcasted_iota(jnp.int32, (POOL_HALO, 1), 0)
    t_lo = t0 + edge
    t_hi = t0 + (tm - POOL_HALO) + edge
    diffs = []
    for gi, w in enumerate(POOL_WINDOWS):
        sl = slice(gi * POOL_GROUP, (gi + 1) * POOL_GROUP)
        mean = _window_sum(u_ext[:, sl], w) * (1.0 / w)
        fixes = []
        for t in (t_lo, t_hi):
            cnt = jnp.minimum(t + (w - w // 2), seq_len) - jnp.maximum(t - w // 2, 0)
            fixes.append(w / cnt.astype(jnp.float32))
        mean = jnp.concatenate([mean[:POOL_HALO] * fixes[0], mean[POOL_HALO:tm - POOL_HALO],
                                mean[tm - POOL_HALO:] * fixes[1]], axis=0)
        diffs.append((mean - u[:, sl]).astype(jnp.bfloat16))
    ys = []
    for j in range(POOL_WIDTH // MXU_DIM):
        d2 = jnp.concatenate(diffs[2 * j:2 * j + 2], axis=-1)
        ys.append(jnp.dot(d2, wp_ref[j], preferred_element_type=jnp.float32))
    y = jnp.concatenate(ys, axis=-1)
    return y * ps_ref[...] * _silu(g)


def _col_rms(x_t, n):
    return lax.rsqrt(jnp.sum(x_t * x_t, axis=0, keepdims=True) * (1.0 / n) + EPS)


def _rotary_t(x_t, cos_t, sin_t):
    half = QK_ROPE // 2
    x1, x2 = x_t[:half], x_t[half:]
    return jnp.concatenate([x1 * cos_t - x2 * sin_t, x2 * cos_t + x1 * sin_t], axis=0)


def _proj_kernel(x_ref, xp_ref, xn_ref, mod_ref, ng_ref, wlat_ref, wrest_ref, gains_ref, wuqt_ref,
                 wukvt_ref, wp_ref, ps_ref, inv_ref, pos_ref,
                 yp_ref, qt_ref, k_ref, vt_ref, ga_ref):
    i = pl.program_id(1)
    tm = x_ref.shape[1]
    seq_len = tm * pl.num_programs(1)
    shift = _mod_row(mod_ref, 0)
    gain = ng_ref[...] * (1.0 + _mod_row(mod_ref, 1))
    gains = lambda rows, ts: jnp.concatenate([gains_ref[rows, :]] * (ts // LANES), axis=1)

    def norm_mod(x):
        return ((x * _rms(x, D_MODEL)) * gain + shift).astype(jnp.bfloat16)

    def matmul_phase(r0, ts):
        rows = slice(r0, r0 + ts)
        xn = norm_mod(x_ref[0, rows, :])
        lat = jnp.dot(xn, wlat_ref[...], preferred_element_type=jnp.float32)
        ug = jnp.dot(xn, wrest_ref[:, :2 * POOL_WIDTH], preferred_element_type=jnp.float32)
        ga_ref[0, rows, :] = jnp.dot(xn, wrest_ref[:, 2 * POOL_WIDTH:],
                                     preferred_element_type=jnp.float32)
        lat_t = lat.T
        q_lat_t = lat_t[:Q_LORA]
        kv_lat_t = lat_t[Q_LORA:Q_LORA + KV_LORA]
        kr_t = lat_t[Q_LORA + KV_LORA:Q_LORA + KV_LORA + QK_ROPE]
        qn_t = (q_lat_t * _col_rms(q_lat_t, Q_LORA) * gains(G_QLAT, ts)).astype(jnp.bfloat16)
        kvn_t = (kv_lat_t * _col_rms(kv_lat_t, KV_LORA)
                 * gains(G_KVLAT, ts)).astype(jnp.bfloat16)
        q_raw_t = jnp.dot(wuqt_ref[...], qn_t, preferred_element_type=jnp.float32)
        kv_t = jnp.dot(wukvt_ref[...], kvn_t, preferred_element_type=jnp.float32)
        vt_ref[0, :, rows] = kv_t[ATTN_WIDTH:].astype(jnp.bfloat16)
        return ug, q_raw_t, kv_t[:ATTN_WIDTH], kr_t

    def vector_phase(r0, ug, u_before, u_after, q_raw_t, k_nope_t, kr_t):
        ts = ug.shape[0]
        rows = slice(r0, r0 + ts)
        yp = _pool_branch(ug[:, :POOL_WIDTH], jnp.concatenate([u_before, u_after], axis=0),
                          ug[:, POOL_WIDTH:], i * tm + r0, seq_len, wp_ref, ps_ref)
        yp_ref[0, rows, :] = yp.astype(jnp.bfloat16)

        ang = pos_ref[0, :, rows].astype(jnp.float32) * inv_ref[...]
        cos_t = jnp.cos(ang)
        sin_t = jnp.sin(ang)
        pad_rows = jnp.zeros((HEAD_PAD - QK_HEAD, ts), jnp.float32)
        qc = QK_HEAD ** -0.5 * math.log2(math.e)
        gqn = gains(G_QNORM, ts)
        for h in range(N_HEADS):
            blk = q_raw_t[h * HEAD_PAD:h * HEAD_PAD + QK_HEAD]
            gb = blk * gqn
            qh_t = jnp.concatenate(
                [gb[:QK_NOPE], _rotary_t(gb[QK_NOPE:], cos_t, sin_t), pad_rows],
                axis=0) * (_col_rms(blk, QK_HEAD) * qc)
            qt_ref[0, h * HEAD_PAD:(h + 1) * HEAD_PAD, rows] = qh_t.astype(jnp.bfloat16)

        gkn = gains(G_KNORM, ts)
        k_rope_t = _rotary_t(kr_t * gkn[QK_NOPE:], cos_t, sin_t)
        kr_ssq = jnp.sum(kr_t * kr_t, axis=0, keepdims=True)
        for h in range(N_HEADS):
            nope = k_nope_t[h * QK_NOPE:(h + 1) * QK_NOPE]
            ssq = jnp.sum(nope * nope, axis=0, keepdims=True) + kr_ssq
            r = lax.rsqrt(ssq * (1.0 / QK_HEAD) + EPS)
            kh_t = jnp.concatenate([nope * gkn[:QK_NOPE], k_rope_t, pad_rows], axis=0) * r
            k_ref[0, rows, h * HEAD_PAD:(h + 1) * HEAD_PAD] = kh_t.T.astype(jnp.bfloat16)

    xh = norm_mod(jnp.concatenate([xp_ref[0], xn_ref[0]], axis=0))
    u_halo = jnp.dot(xh, wrest_ref[:, :POOL_WIDTH], preferred_element_type=jnp.float32)
    row = lax.broadcasted_iota(jnp.int32, (2 * POOL_HALO, 1), 0)
    has_prev = (i > 0).astype(jnp.float32)
    has_next = (i < pl.num_programs(1) - 1).astype(jnp.float32)
    u_halo = u_halo * jnp.where(row < POOL_HALO, has_prev, has_next)

    starts = [sum(PROJ_SUBTILES[:n]) for n in range(len(PROJ_SUBTILES))]
    n_sub = len(PROJ_SUBTILES)
    phases = [matmul_phase(starts[0], PROJ_SUBTILES[0])]
    for n in range(n_sub):
        if n + 1 < n_sub:
            phases.append(matmul_phase(starts[n + 1], PROJ_SUBTILES[n + 1]))
        ug, q_raw_t, k_nope_t, kr_t = phases[n]
        u_before = (u_halo[:POOL_HALO] if n == 0
                    else phases[n - 1][0][PROJ_SUBTILES[n - 1] - POOL_HALO:, :POOL_WIDTH])
        u_after = (u_halo[POOL_HALO:] if n + 1 == n_sub
                   else phases[n + 1][0][:POOL_HALO, :POOL_WIDTH])
        vector_phase(starts[n], ug, u_before, u_after, q_raw_t, k_nope_t, kr_t)


def _proj_stage(x, mod, consts, positions):
    b, s, d = x.shape
    tm = PROJ_TILE
    hp = N_HEADS * HEAD_PAD
    halo_blocks = tm // POOL_HALO
    last_block = s // POOL_HALO - 1
    tok = lambda w: pl.BlockSpec((1, tm, w), lambda bi, i: (bi, i, 0))
    full = lambda a: pl.BlockSpec(a.shape, lambda bi, i: (0,) * a.ndim)
    pos = pl.BlockSpec((1, 1, tm), lambda bi, i: (bi, 0, i))
    prev = pl.BlockSpec((1, POOL_HALO, d),
                        lambda bi, i: (bi, jnp.maximum(i * halo_blocks - 1, 0), 0))
    nxt = pl.BlockSpec((1, POOL_HALO, d),
                       lambda bi, i: (bi, jnp.minimum((i + 1) * halo_blocks, last_block), 0))
    return pl.pallas_call(
        _proj_kernel,
        grid=(b, s // tm),
        in_specs=[tok(d), prev, nxt, full(mod)] + [full(a) for a in consts] + [pos],
        out_specs=[tok(POOL_WIDTH), pl.BlockSpec((1, hp, tm), lambda bi, i: (bi, 0, i)), tok(hp),
                   pl.BlockSpec((1, ATTN_WIDTH, tm), lambda bi, i: (bi, 0, i)), tok(ATTN_WIDTH)],
        out_shape=[jax.ShapeDtypeStruct((b, s, POOL_WIDTH), jnp.bfloat16),
                   jax.ShapeDtypeStruct((b, hp, s), jnp.bfloat16),
                   jax.ShapeDtypeStruct((b, s, hp), jnp.bfloat16),
                   jax.ShapeDtypeStruct((b, ATTN_WIDTH, s), jnp.bfloat16),
                   jax.ShapeDtypeStruct((b, s, ATTN_WIDTH), jnp.float32)],
        compiler_params=pltpu.CompilerParams(
            dimension_semantics=("arbitrary", "arbitrary"), vmem_limit_bytes=VMEM_LIMIT),
        name="proj_stage",
    )(x, x, x, mod, *consts, positions.reshape(b, 1, s))


def _attn_body(shift_max, qt_ref, k_ref, vt_ref, ga_ref, yp_ref, x_ref, mod_ref, wo_ref, o_ref):
    kc = k_ref.shape[1] if shift_max else KEY_CHUNK
    tq = ATTN_SUBTILE_SHIFTED if shift_max else ATTN_SUBTILE
    n_chunks = k_ref.shape[1] // kc
    steps = [(q0, h, j) for q0 in range(0, qt_ref.shape[2], tq) for h in range(N_HEADS)
             for j in range(n_chunks)]

    def scores_t(q0, h, j):
        sl = slice(h * HEAD_PAD, (h + 1) * HEAD_PAD)
        kh = k_ref[0, j * kc:(j + 1) * kc, sl]
        qh = qt_ref[0, sl, q0:q0 + tq]
        return jnp.dot(kh, qh, preferred_element_type=jnp.float32)

    def finish(q0, outs):
        rows = slice(q0, q0 + tq)
        o = jnp.concatenate(outs, axis=0).T
        ya = (o * _silu(ga_ref[0, rows, :])).astype(jnp.bfloat16)
        ycat = jnp.concatenate([yp_ref[0, rows, :], ya], axis=-1)
        y = jnp.dot(ycat, wo_ref[...], preferred_element_type=jnp.float32)
        o_ref[0, rows, :] = x_ref[0, rows, :] + _mod_row(mod_ref, 2) * y

    outs = []
    ahead = [scores_t(*steps[n]) for n in range(min(SCORE_LOOKAHEAD, len(steps)))]
    for n, (q0, h, j) in enumerate(steps):
        st = ahead.pop(0)
        if n + SCORE_LOOKAHEAD < len(steps):
            ahead.append(scores_t(*steps[n + SCORE_LOOKAHEAD]))
        if shift_max:
            st = st - jnp.max(st, axis=0, keepdims=True)
        p = jnp.exp2(st)
        l_c = jnp.sum(p, axis=0, keepdims=True)
        o_c = jnp.dot(vt_ref[0, h * V_HEAD:(h + 1) * V_HEAD, j * kc:(j + 1) * kc],
                      p.astype(jnp.bfloat16), preferred_element_type=jnp.float32)
        l, ot = (l_c, o_c) if j == 0 else (l + l_c, ot + o_c)
        if j + 1 == n_chunks:
            outs.append(ot / l)
            if h + 1 == N_HEADS:
                finish(q0, outs)
                outs = []


def _attn_kernel(bounded_ref, *refs):
    @pl.when(bounded_ref[0] != 0)
    def _():
        _attn_body(False, *refs)

    @pl.when(bounded_ref[0] == 0)
    def _():
        _attn_body(True, *refs)


def _attn_stage(bounded, q, k, vt, g_attn, y_pool, x, mod, w_out_bf):
    b, s, d = x.shape
    tq = ATTN_TILE
    hp = N_HEADS * HEAD_PAD
    tok = lambda w: pl.BlockSpec((1, tq, w), lambda bi, i: (bi, i, 0))
    return pl.pallas_call(
        _attn_kernel,
        grid=(b, s // tq),
        in_specs=[pl.BlockSpec(memory_space=pltpu.SMEM),
                  pl.BlockSpec((1, hp, tq), lambda bi, i: (bi, 0, i)),
                  pl.BlockSpec((1, s, hp), lambda bi, i: (bi, 0, 0)),
                  pl.BlockSpec((1, ATTN_WIDTH, s), lambda bi, i: (bi, 0, 0)),
                  tok(ATTN_WIDTH), tok(POOL_WIDTH), tok(d),
                  pl.BlockSpec(mod.shape, lambda bi, i: (0, 0)),
                  pl.BlockSpec(w_out_bf.shape, lambda bi, i: (0, 0))],
        out_specs=tok(d),
        out_shape=jax.ShapeDtypeStruct((b, s, d), jnp.float32),
        compiler_params=pltpu.CompilerParams(
            dimension_semantics=("arbitrary", "arbitrary"), vmem_limit_bytes=VMEM_LIMIT),
        name="attn_out_stage",
    )(bounded, q, k, vt, g_attn, y_pool, x, mod, w_out_bf)


def _pad_heads(w, width):
    kdim = w.shape[0]
    w = w.reshape(kdim, N_HEADS, width)
    w = jnp.pad(w, ((0, 0), (0, 0), (0, HEAD_PAD - width)))
    return w.reshape(kdim, N_HEADS * HEAD_PAD)


def kernel(x, c, positions, ada_w, ada_b, norm_g, w_in, pool_w, pool_scale, g_q_lat, w_uq,
           g_kv_lat, w_ukv, g_qnorm, g_knorm, w_out):
    b, s, d = x.shape
    bf = jnp.bfloat16
    mod = _ada_mod(c, ada_w[0], ada_b[0])

    wi = w_in[0]
    o_q, o_ga = 2 * POOL_WIDTH, sum(IN_SPLITS[:5])
    w_lat = jnp.concatenate(
        [wi[:, o_q:o_ga], jnp.zeros((d, LAT_WIDTH - (o_ga - o_q)), wi.dtype)], axis=1).astype(bf)
    w_rest = jnp.concatenate([wi[:, :o_q], wi[:, o_ga:]], axis=1).astype(bf)
    w_uqt_p = _pad_heads(w_uq[0], QK_HEAD).T.astype(bf)
    wkv = w_ukv[0].reshape(KV_LORA, N_HEADS, QK_NOPE + V_HEAD)
    w_ukvt = jnp.concatenate([wkv[..., :QK_NOPE].reshape(KV_LORA, -1),
                              wkv[..., QK_NOPE:].reshape(KV_LORA, -1)], axis=1).T.astype(bf)
    g_rows = jnp.concatenate([g_q_lat[0], g_kv_lat[0], g_qnorm[0], g_knorm[0]])
    gains = jnp.broadcast_to(g_rows.reshape(-1, 1), (g_rows.shape[0], LANES))
    pw = pool_w[0].astype(bf)
    zero = jnp.zeros((POOL_GROUP, POOL_GROUP), bf)
    w_pool = jnp.stack([jnp.block([[pw[2 * j], zero], [zero, pw[2 * j + 1]]])
                        for j in range(POOL_WIDTH // MXU_DIM)])
    inv_freq = ROPE_THETA ** (-jnp.arange(0, QK_ROPE, 2, dtype=jnp.float32) / QK_ROPE)
    consts = (norm_g[0].reshape(1, d), w_lat, w_rest, gains, w_uqt_p, w_ukvt, w_pool,
              pool_scale[0].reshape(1, POOL_WIDTH), inv_freq.reshape(QK_ROPE // 2, 1))

    y_pool, qt, k, vt, g_attn = _proj_stage(x, mod, consts, positions)
    score_bound = (math.log2(math.e) * QK_HEAD ** 0.5
                   * jnp.max(jnp.abs(g_qnorm[0])) * jnp.max(jnp.abs(g_knorm[0])))
    bounded = (score_bound <= SCORE_LIMIT).astype(jnp.int32).reshape(1)
    return _attn_stage(bounded, qt, k, vt, g_attn, y_pool, x, mod, w_out[0].astype(bf))
```

```python
import math

import jax
import jax.numpy as jnp
from jax import lax
from jax.experimental import pallas as pl
from jax.experimental.pallas import tpu as pltpu

D_MODEL = 1024
POOL_WINDOWS = (2, 4, 8, 16)
POOL_WIDTH = 512
POOL_GROUP = 128
N_HEADS = 8
QK_NOPE = 64
QK_ROPE = 32
QK_HEAD = 96
V_HEAD = 64
ATTN_WIDTH = 512
Q_LORA = 256
KV_LORA = 128
IN_SPLITS = (512, 512, 256, 128, 32, 512)
ROPE_THETA = 10000.0
EPS = 1e-6

LANES = 128
SUBLANES = 8
HEAD_PAD = LANES
MXU_DIM = 256
POOL_HALO = max(POOL_WINDOWS) // 2
VMEM_LIMIT = 56 * 1024 * 1024
LAT_WIDTH = 512
SCORE_LIMIT = 64.0
PROJ_TILE = 1024
PROJ_SUBTILES = (256, 256, 256, 256)
ATTN_TILE = 512
ATTN_SUBTILE = 512
ATTN_SUBTILE_SHIFTED = 256
KEY_CHUNK = 256
SCORE_LOOKAHEAD = 2

G_QLAT = slice(0, Q_LORA)
G_KVLAT = slice(Q_LORA, Q_LORA + KV_LORA)
G_QNORM = slice(Q_LORA + KV_LORA, Q_LORA + KV_LORA + QK_HEAD)
G_KNORM = slice(Q_LORA + KV_LORA + QK_HEAD, Q_LORA + KV_LORA + 2 * QK_HEAD)

assert POOL_HALO == SUBLANES


def _silu(x):
    return x * jax.nn.sigmoid(x)


def _mod_kernel(c_ref, w_ref, b_ref, o_ref):
    c_act = _silu(c_ref[...])
    o_ref[...] = jnp.dot(c_act, w_ref[...], preferred_element_type=jnp.float32,
                         precision=lax.Precision.HIGHEST) + b_ref[...]


def _ada_mod(c, ada_w, ada_b):
    b, d = c.shape
    n = ada_w.shape[1]
    tn = 512
    return pl.pallas_call(
        _mod_kernel,
        grid=(n // tn,),
        in_specs=[pl.BlockSpec((b, d), lambda j: (0, 0)),
                  pl.BlockSpec((d, tn), lambda j: (0, j)),
                  pl.BlockSpec((1, tn), lambda j: (0, j))],
        out_specs=pl.BlockSpec((b, tn), lambda j: (0, j)),
        out_shape=jax.ShapeDtypeStruct((b, n), jnp.float32),
        name="ada_mod",
    )(c, ada_w, ada_b.reshape(1, n))


def _mod_row(mod_ref, part):
    return mod_ref[pl.ds(pl.program_id(0), 1), part * D_MODEL:(part + 1) * D_MODEL]


def _rms(x, n):
    return lax.rsqrt(jnp.sum(x * x, axis=-1, keepdims=True) * (1.0 / n) + EPS)


def _window_sum(u_ext, w):
    n = u_ext.shape[0]
    tm = n - 2 * POOL_HALO
    ahead = lambda a, k: pltpu.roll(a, n - k, 0)
    behind = lambda a, k: pltpu.roll(a, k, 0)
    inner = lambda a: a[POOL_HALO:POOL_HALO + tm]
    if w == 2:
        return inner(u_ext + behind(u_ext, 1))
    fwd = u_ext + ahead(u_ext, 1)
    span = 2
    while 2 * span < w:
        fwd = fwd + ahead(fwd, span)
        span *= 2
    if span == POOL_HALO:
        return fwd[:tm] + inner(fwd)
    return inner(fwd + behind(fwd, span))


def _pool_branch(u, u_halo, g, t0, seq_len, wp_ref, ps_ref):
    tm = u.shape[0]
    u_ext = jnp.concatenate([u_halo[:POOL_HALO], u, u_halo[POOL_HALO:]], axis=0)
    edge = lax.broadcasted_iota(jnp.int32, (POOL_HALO, 1), 0)
    t_lo = t0 + edge
    t_hi = t0 + (tm - POOL_HALO) + edge
    diffs = []
    for gi, w in enumerate(POOL_WINDOWS):
        sl = slice(gi * POOL_GROUP, (gi + 1) * POOL_GROUP)
        mean = _window_sum(u_ext[:, sl], w) * (1.0 / w)
        fixes = []
        for t in (t_lo, t_hi):
            cnt = jnp.minimum(t + (w - w // 2), seq_len) - jnp.maximum(t - w // 2, 0)
            fixes.append(w / cnt.astype(jnp.float32))
        mean = jnp.concatenate([mean[:POOL_HALO] * fixes[0], mean[POOL_HALO:tm - POOL_HALO],
                                mean[tm - POOL_HALO:] * fixes[1]], axis=0)
        diffs.append((mean - u[:, sl]).astype(jnp.bfloat16))
    ys = []
    for j in range(POOL_WIDTH // MXU_DIM):
        d2 = jnp.concatenate(diffs[2 * j:2 * j + 2], axis=-1)
        ys.append(jnp.dot(d2, wp_ref[j], preferred_element_type=jnp.float32))
    y = jnp.concatenate(ys, axis=-1)
    return y * ps_ref[...] * _silu(g)


def _col_rms(x_t, n):
    return lax.rsqrt(jnp.sum(x_t * x_t, axis=0, keepdims=True) * (1.0 / n) + EPS)


def _rotary_t(x_t, cos_t, sin_t):
    half = QK_ROPE // 2
    x1, x2 = x_t[:half], x_t[half:]
    return jnp.concatenate([x1 * cos_t - x2 * sin_t, x2 * cos_t + x1 * sin_t], axis=0)


def _proj_kernel(x_ref, xp_ref, xn_ref, mod_ref, ng_ref, wlat_ref, wrest_ref, gains_ref, wuqt_ref,
                 wukvt_ref, wp_ref, ps_ref, inv_ref, pos_ref,
                 yp_ref, qt_ref, k_ref, vt_ref, ga_ref):
    i = pl.program_id(1)
    tm = x_ref.shape[1]
    seq_len = tm * pl.num_programs(1)
    shift = _mod_row(mod_ref, 0)
    gain = ng_ref[...] * (1.0 + _mod_row(mod_ref, 1))
    gains = lambda rows, ts: jnp.concatenate([gains_ref[rows, :]] * (ts // LANES), axis=1)

    def norm_mod(x):
        return ((x * _rms(x, D_MODEL)) * gain + shift).astype(jnp.bfloat16)

    def matmul_phase(r0, ts):
        rows = slice(r0, r0 + ts)
        xn = norm_mod(x_ref[0, rows, :])
        lat = jnp.dot(xn, wlat_ref[...], preferred_element_type=jnp.float32)
        ug = jnp.dot(xn, wrest_ref[:, :2 * POOL_WIDTH], preferred_element_type=jnp.float32)
        ga_ref[0, rows, :] = jnp.dot(xn, wrest_ref[:, 2 * POOL_WIDTH:],
                                     preferred_element_type=jnp.float32)
        lat_t = lat.T
        q_lat_t = lat_t[:Q_LORA]
        kv_lat_t = lat_t[Q_LORA:Q_LORA + KV_LORA]
        kr_t = lat_t[Q_LORA + KV_LORA:Q_LORA + KV_LORA + QK_ROPE]
        qn_t = (q_lat_t * _col_rms(q_lat_t, Q_LORA) * gains(G_QLAT, ts)).astype(jnp.bfloat16)
        kvn_t = (kv_lat_t * _col_rms(kv_lat_t, KV_LORA)
                 * gains(G_KVLAT, ts)).astype(jnp.bfloat16)
        q_raw_t = jnp.dot(wuqt_ref[...], qn_t, preferred_element_type=jnp.float32)
        kv_t = jnp.dot(wukvt_ref[...], kvn_t, preferred_element_type=jnp.float32)
        vt_ref[0, :, rows] = kv_t[ATTN_WIDTH:].astype(jnp.bfloat16)
        return ug, q_raw_t, kv_t[:ATTN_WIDTH], kr_t

    def vector_phase(r0, ug, u_before, u_after, q_raw_t, k_nope_t, kr_t):
        ts = ug.shape[0]
        rows = slice(r0, r0 + ts)
        yp = _pool_branch(ug[:, :POOL_WIDTH], jnp.concatenate([u_before, u_after], axis=0),
                          ug[:, POOL_WIDTH:], i * tm + r0, seq_len, wp_ref, ps_ref)
        yp_ref[0, rows, :] = yp.astype(jnp.bfloat16)

        pos = pos_ref[pl.ds(pl.program_id(0), 1), rows]
        ang = pos.astype(jnp.float32) * inv_ref[...]
        cos_t = jnp.cos(ang)
        sin_t = jnp.sin(ang)
        pad_rows = jnp.zeros((HEAD_PAD - QK_HEAD, ts), jnp.float32)
        qc = QK_HEAD ** -0.5 * math.log2(math.e)
        gqn = gains(G_QNORM, ts)
        for h in range(N_HEADS):
            blk = q_raw_t[h * HEAD_PAD:h * HEAD_PAD + QK_HEAD]
            gb = blk * gqn
            qh_t = jnp.concatenate(
                [gb[:QK_NOPE], _rotary_t(gb[QK_NOPE:], cos_t, sin_t), pad_rows],
                axis=0) * (_col_rms(blk, QK_HEAD) * qc)
            qt_ref[0, h * HEAD_PAD:(h + 1) * HEAD_PAD, rows] = qh_t.astype(jnp.bfloat16)

        gkn = gains(G_KNORM, ts)
        k_rope_t = _rotary_t(kr_t * gkn[QK_NOPE:], cos_t, sin_t)
        kr_ssq = jnp.sum(kr_t * kr_t, axis=0, keepdims=True)
        for h in range(N_HEADS):
            nope = k_nope_t[h * QK_NOPE:(h + 1) * QK_NOPE]
            ssq = jnp.sum(nope * nope, axis=0, keepdims=True) + kr_ssq
            r = lax.rsqrt(ssq * (1.0 / QK_HEAD) + EPS)
            kh_t = jnp.concatenate([nope * gkn[:QK_NOPE], k_rope_t, pad_rows], axis=0) * r
            k_ref[0, rows, h * HEAD_PAD:(h + 1) * HEAD_PAD] = kh_t.T.astype(jnp.bfloat16)

    xh = norm_mod(jnp.concatenate([xp_ref[0], xn_ref[0]], axis=0))
    u_halo = jnp.dot(xh, wrest_ref[:, :POOL_WIDTH], preferred_element_type=jnp.float32)
    row = lax.broadcasted_iota(jnp.int32, (2 * POOL_HALO, 1), 0)
    has_prev = (i > 0).astype(jnp.float32)
    has_next = (i < pl.num_programs(1) - 1).astype(jnp.float32)
    u_halo = u_halo * jnp.where(row < POOL_HALO, has_prev, has_next)

    starts = [sum(PROJ_SUBTILES[:n]) for n in range(len(PROJ_SUBTILES))]
    n_sub = len(PROJ_SUBTILES)
    phases = [matmul_phase(starts[0], PROJ_SUBTILES[0])]
    for n in range(n_sub):
        if n + 1 < n_sub:
            phases.append(matmul_phase(starts[n + 1], PROJ_SUBTILES[n + 1]))
        ug, q_raw_t, k_nope_t, kr_t = phases[n]
        u_before = (u_halo[:POOL_HALO] if n == 0
                    else phases[n - 1][0][PROJ_SUBTILES[n - 1] - POOL_HALO:, :POOL_WIDTH])
        u_after = (u_halo[POOL_HALO:] if n + 1 == n_sub
                   else phases[n + 1][0][:POOL_HALO, :POOL_WIDTH])
        vector_phase(starts[n], ug, u_before, u_after, q_raw_t, k_nope_t, kr_t)


def _proj_stage(x, mod, consts, positions):
    b, s, d = x.shape
    tm = PROJ_TILE
    hp = N_HEADS * HEAD_PAD
    halo_blocks = tm // POOL_HALO
    last_block = s // POOL_HALO - 1
    tok = lambda w: pl.BlockSpec((1, tm, w), lambda bi, i: (bi, i, 0))
    full = lambda a: pl.BlockSpec(a.shape, lambda bi, i: (0,) * a.ndim)
    pos = pl.BlockSpec((b, tm), lambda bi, i: (0, i))
    prev = pl.BlockSpec((1, POOL_HALO, d),
                        lambda bi, i: (bi, jnp.maximum(i * halo_blocks - 1, 0), 0))
    nxt = pl.BlockSpec((1, POOL_HALO, d),
                       lambda bi, i: (bi, jnp.minimum((i + 1) * halo_blocks, last_block), 0))
    return pl.pallas_call(
        _proj_kernel,
        grid=(b, s // tm),
        in_specs=[tok(d), prev, nxt, full(mod)] + [full(a) for a in consts] + [pos],
        out_specs=[tok(POOL_WIDTH), pl.BlockSpec((1, hp, tm), lambda bi, i: (bi, 0, i)), tok(hp),
                   pl.BlockSpec((1, ATTN_WIDTH, tm), lambda bi, i: (bi, 0, i)), tok(ATTN_WIDTH)],
        out_shape=[jax.ShapeDtypeStruct((b, s, POOL_WIDTH), jnp.bfloat16),
                   jax.ShapeDtypeStruct((b, hp, s), jnp.bfloat16),
                   jax.ShapeDtypeStruct((b, s, hp), jnp.bfloat16),
                   jax.ShapeDtypeStruct((b, ATTN_WIDTH, s), jnp.bfloat16),
                   jax.ShapeDtypeStruct((b, s, ATTN_WIDTH), jnp.float32)],
        compiler_params=pltpu.CompilerParams(
            dimension_semantics=("arbitrary", "arbitrary"), vmem_limit_bytes=VMEM_LIMIT),
        name="proj_stage",
    )(x, x, x, mod, *consts, positions)


def _attn_body(shift_max, qt_ref, k_ref, vt_ref, ga_ref, yp_ref, x_ref, mod_ref, wo_ref, o_ref):
    kc = k_ref.shape[1] if shift_max else KEY_CHUNK
    tq = ATTN_SUBTILE_SHIFTED if shift_max else ATTN_SUBTILE
    n_chunks = k_ref.shape[1] // kc
    steps = [(q0, h, j) for q0 in range(0, qt_ref.shape[2], tq) for h in range(N_HEADS)
             for j in range(n_chunks)]

    def scores_t(q0, h, j):
        sl = slice(h * HEAD_PAD, (h + 1) * HEAD_PAD)
        kh = k_ref[0, j * kc:(j + 1) * kc, sl]
        qh = qt_ref[0, sl, q0:q0 + tq]
        return jnp.dot(kh, qh, preferred_element_type=jnp.float32)

    def finish(q0, outs):
        rows = slice(q0, q0 + tq)
        o = jnp.concatenate(outs, axis=0).T
        ya = (o * _silu(ga_ref[0, rows, :])).astype(jnp.bfloat16)
        ycat = jnp.concatenate([yp_ref[0, rows, :], ya], axis=-1)
        y = jnp.dot(ycat, wo_ref[...], preferred_element_type=jnp.float32)
        o_ref[0, rows, :] = x_ref[0, rows, :] + _mod_row(mod_ref, 2) * y

    outs = []
    ahead = [scores_t(*steps[n]) for n in range(min(SCORE_LOOKAHEAD, len(steps)))]
    for n, (q0, h, j) in enumerate(steps):
        st = ahead.pop(0)
        if n + SCORE_LOOKAHEAD < len(steps):
            ahead.append(scores_t(*steps[n + SCORE_LOOKAHEAD]))
        if shift_max:
            st = st - jnp.max(st, axis=0, keepdims=True)
        p = jnp.exp2(st)
        l_c = jnp.sum(p, axis=0, keepdims=True)
        o_c = jnp.dot(vt_ref[0, h * V_HEAD:(h + 1) * V_HEAD, j * kc:(j + 1) * kc],
                      p.astype(jnp.bfloat16), preferred_element_type=jnp.float32)
        l, ot = (l_c, o_c) if j == 0 else (l + l_c, ot + o_c)
        if j + 1 == n_chunks:
            outs.append(ot / l)
            if h + 1 == N_HEADS:
                finish(q0, outs)
                outs = []


def _attn_kernel(bounded_ref, *refs):
    @pl.when(bounded_ref[0] != 0)
    def _():
        _attn_body(False, *refs)

    @pl.when(bounded_ref[0] == 0)
    def _():
        _attn_body(True, *refs)


def _attn_stage(bounded, q, k, vt, g_attn, y_pool, x, mod, w_out_bf):
    b, s, d = x.shape
    tq = ATTN_TILE
    hp = N_HEADS * HEAD_PAD
    tok = lambda w: pl.BlockSpec((1, tq, w), lambda bi, i: (bi, i, 0))
    return pl.pallas_call(
        _attn_kernel,
        grid=(b, s // tq),
        in_specs=[pl.BlockSpec(memory_space=pltpu.SMEM),
                  pl.BlockSpec((1, hp, tq), lambda bi, i: (bi, 0, i)),
                  pl.BlockSpec((1, s, hp), lambda bi, i: (bi, 0, 0)),
                  pl.BlockSpec((1, ATTN_WIDTH, s), lambda bi, i: (bi, 0, 0)),
                  tok(ATTN_WIDTH), tok(POOL_WIDTH), tok(d),
                  pl.BlockSpec(mod.shape, lambda bi, i: (0, 0)),
                  pl.BlockSpec(w_out_bf.shape, lambda bi, i: (0, 0))],
        out_specs=tok(d),
        out_shape=jax.ShapeDtypeStruct((b, s, d), jnp.float32),
        compiler_params=pltpu.CompilerParams(
            dimension_semantics=("arbitrary", "arbitrary"), vmem_limit_bytes=VMEM_LIMIT),
        name="attn_out_stage",
    )(bounded, q, k, vt, g_attn, y_pool, x, mod, w_out_bf)


def _pad_heads(w, width):
    kdim = w.shape[0]
    w = w.reshape(kdim, N_HEADS, width)
    w = jnp.pad(w, ((0, 0), (0, 0), (0, HEAD_PAD - width)))
    return w.reshape(kdim, N_HEADS * HEAD_PAD)


def kernel(x, c, positions, ada_w, ada_b, norm_g, w_in, pool_w, pool_scale, g_q_lat, w_uq,
           g_kv_lat, w_ukv, g_qnorm, g_knorm, w_out):
    b, s, d = x.shape
    bf = jnp.bfloat16
    mod = _ada_mod(c, ada_w[0], ada_b[0])

    wi = w_in[0]
    o_q, o_ga = 2 * POOL_WIDTH, sum(IN_SPLITS[:5])
    w_lat = jnp.concatenate(
        [wi[:, o_q:o_ga], jnp.zeros((d, LAT_WIDTH - (o_ga - o_q)), wi.dtype)], axis=1).astype(bf)
    w_rest = jnp.concatenate([wi[:, :o_q], wi[:, o_ga:]], axis=1).astype(bf)
    w_uqt_p = _pad_heads(w_uq[0], QK_HEAD).T.astype(bf)
    wkv = w_ukv[0].reshape(KV_LORA, N_HEADS, QK_NOPE + V_HEAD)
    w_ukvt = jnp.concatenate([wkv[..., :QK_NOPE].reshape(KV_LORA, -1),
                              wkv[..., QK_NOPE:].reshape(KV_LORA, -1)], axis=1).T.astype(bf)
    g_rows = jnp.concatenate([g_q_lat[0], g_kv_lat[0], g_qnorm[0], g_knorm[0]])
    gains = jnp.broadcast_to(g_rows.reshape(-1, 1), (g_rows.shape[0], LANES))
    pw = pool_w[0].astype(bf)
    zero = jnp.zeros((POOL_GROUP, POOL_GROUP), bf)
    w_pool = jnp.stack([jnp.block([[pw[2 * j], zero], [zero, pw[2 * j + 1]]])
                        for j in range(POOL_WIDTH // MXU_DIM)])
    inv_freq = ROPE_THETA ** (-jnp.arange(0, QK_ROPE, 2, dtype=jnp.float32) / QK_ROPE)
    consts = (norm_g[0].reshape(1, d), w_lat, w_rest, gains, w_uqt_p, w_ukvt, w_pool,
              pool_scale[0].reshape(1, POOL_WIDTH), inv_freq.reshape(QK_ROPE // 2, 1))

    y_pool, qt, k, vt, g_attn = _proj_stage(x, mod, consts, positions)
    g_max = jnp.max(jnp.abs(jnp.stack([g_qnorm[0], g_knorm[0]])), axis=1)
    score_bound = math.log2(math.e) * QK_HEAD ** 0.5 * g_max[0] * g_max[1]
    bounded = (score_bound <= SCORE_LIMIT).astype(jnp.int32).reshape(1)
    return _attn_stage(bounded, qt, k, vt, g_attn, y_pool, x, mod, w_out[0].astype(bf))
```
